```python
import math
import jax
import jax.numpy as jnp
from jax import lax
import numpy as np

D_MODEL = 1024
BATCH = 32
SEQ = 2048
DEPTH = 2

D_CONV = D_MODEL // 4
CONV_GROUPS = 4
CONV_WIDTH = 3
D_POOL = D_MODEL // 4
POOL_WINDOWS = (2, 4, 8, 16)
N_POOL = len(POOL_WINDOWS)
POOL_GROUP = D_POOL // N_POOL
D_ATTN = D_MODEL - D_CONV - D_POOL
HEAD_DIM = 64
N_HEADS = D_ATTN // HEAD_DIM
N_KV = 2
HPG = N_HEADS // N_KV
D_KV = N_KV * HEAD_DIM
D_MIX = D_CONV + D_ATTN + D_POOL

CMP_BLOCK = 32
CMP_STRIDE = 16
SEL_BLOCK = 64
TOP_N = 8
WINDOW = 512
N_BRANCH = 3
Q_BLOCK = 64

NUM_BUCKETS = 32
MAX_DISTANCE = 128

D_FF = ((8 * D_MODEL // 3 + 255) // 256) * 256
ALPHA = (2 * DEPTH) ** 0.25
BETA = (8 * DEPTH) ** -0.25
LN_EPS = 1e-5
NEG = -1e30
FORCE = 1e6

SPLIT_SIZES = (D_CONV, D_CONV, D_CONV, D_ATTN, D_KV, D_KV, D_KV, D_KV, D_KV, D_KV, N_BRANCH * N_HEADS, D_POOL)
D_IN = sum(SPLIT_SIZES)
SPLIT_POINTS = tuple(sum(SPLIT_SIZES[:i + 1]) for i in range(len(SPLIT_SIZES) - 1))

kernel_name = "hymba_style_conv_nsa_pool_hybrid"


def layer_norm(x, g, b):
    xf = x.astype(jnp.float32)
    mu = jnp.mean(xf, axis=-1, keepdims=True)
    var = jnp.mean(jnp.square(xf - mu), axis=-1, keepdims=True)
    return ((xf - mu) * lax.rsqrt(var + LN_EPS) * g + b).astype(x.dtype)


def t5_bucket(dist):
    n = jnp.maximum(dist, 0)
    max_exact = NUM_BUCKETS // 2
    nf = jnp.maximum(n, 1).astype(jnp.float32)
    large = max_exact + (jnp.log(nf / max_exact) / math.log(MAX_DISTANCE / max_exact)
                         * (NUM_BUCKETS - max_exact)).astype(jnp.int32)
    large = jnp.minimum(large, NUM_BUCKETS - 1)
    return jnp.where(n < max_exact, n, large)


def masked_softmax(s, valid):
    p = jax.nn.softmax(jnp.where(valid, s.astype(jnp.float32), NEG), axis=-1)
    return jnp.where(valid, p, 0.0)


def short_conv_mixer(b_gate, c_gate, x_conv, conv_w):
    u = c_gate * x_conv
    y = lax.conv_general_dilated(u, conv_w[:, None, :], window_strides=(1,),
                                 padding=[(CONV_WIDTH - 1, 0)],
                                 dimension_numbers=('NWC', 'WIO', 'NWC'),
                                 feature_group_count=D_CONV)
    return b_gate * y


def pool_mixer(u, pool_w, pool_scale):
    B, S, _ = u.shape
    uf = u.astype(jnp.float32)
    cs = jnp.pad(jnp.cumsum(uf, axis=1), ((0, 0), (1, 0), (0, 0)))
    t = jnp.arange(1, S + 1)
    groups = []
    for gi, w in enumerate(POOL_WINDOWS):
        sl = slice(gi * POOL_GROUP, (gi + 1) * POOL_GROUP)
        c = cs[..., sl]
        mean = (c[:, 1:] - jnp.take(c, jnp.maximum(t - w, 0), axis=1)) \
            / jnp.minimum(t, w).astype(jnp.float32)[:, None]
        groups.append(mean - uf[..., sl])
    d = jnp.stack(groups, axis=2).astype(u.dtype)
    y = jnp.einsum('bsgc,gcd->bsgd', d, pool_w) * pool_scale.reshape(N_POOL, POOL_GROUP)
    return y.reshape(B, S, D_POOL)


def nsa_mixer(q, k_cmp, v_cmp, k_slc, v_slc, k_win, v_win, gate_logits,
              cmp_pe, cmp_w1, cmp_w2, rel_bias):
    B, S, _ = q.shape
    nq = S // Q_BLOCK
    nc = (S - CMP_BLOCK) // CMP_STRIDE + 1
    nb = S // SEL_BLOCK
    n_sel = min(TOP_N, nb)
    scale = HEAD_DIM ** -0.5

    def heads_kv(t):
        return t.reshape(B, S, N_KV, HEAD_DIM).transpose(0, 2, 1, 3)

    def q_blocks(t, d):
        return t.reshape(B, nq, Q_BLOCK, N_KV, HPG, d).transpose(1, 0, 3, 4, 2, 5)

    qb_all = q_blocks(q, HEAD_DIM)
    gb_all = q_blocks(jax.nn.sigmoid(gate_logits), N_BRANCH)

    blk = np.arange(nc)[:, None] * CMP_STRIDE + np.arange(CMP_BLOCK)[None, :]

    def compress(kv, pe, w1, w2):
        z = (kv[:, :, blk] + pe).reshape(B, N_KV, nc, CMP_BLOCK * HEAD_DIM)
        return jnp.einsum('bgne,ed->bgnd', jax.nn.gelu(jnp.einsum('bgnf,fe->bgne', z, w1)), w2)

    kc = compress(heads_kv(k_cmp), cmp_pe[0], cmp_w1[0], cmp_w2[0])
    vc = compress(heads_kv(v_cmp), cmp_pe[1], cmp_w1[1], cmp_w2[1])
    cmp_start = np.arange(nc) * CMP_STRIDE
    cmp_end = jnp.asarray(cmp_start + CMP_BLOCK - 1, jnp.int32)

    sel_start = np.arange(nb) * SEL_BLOCK
    ov = np.clip(np.minimum(cmp_start[:, None] + CMP_BLOCK, sel_start[None, :] + SEL_BLOCK)
                 - np.maximum(cmp_start[:, None], sel_start[None, :]), 0, None) / CMP_STRIDE
    overlap = jnp.asarray(ov, jnp.float32)

    ks = heads_kv(k_slc).reshape(B, N_KV, nb, SEL_BLOCK, HEAD_DIM)
    vs = heads_kv(v_slc).reshape(B, N_KV, nb, SEL_BLOCK, HEAD_DIM)
    pad = ((0, 0), (0, 0), (WINDOW, 0), (0, 0))
    kw = jnp.pad(heads_kv(k_win), pad)
    vw = jnp.pad(heads_kv(v_win), pad)

    head_ids = jnp.arange(N_HEADS).reshape(N_KV, HPG)
    bias_flat = rel_bias.T.reshape(-1)

    def head_bias(dist):
        return rel_bias[t5_bucket(dist)].transpose(2, 0, 1).reshape(N_KV, HPG, *dist.shape)

    gather_blocks = jax.vmap(jax.vmap(lambda blocks, ix: blocks[ix]))

    def attend(args):
        qb, gb, s0 = args
        t = s0 + jnp.arange(Q_BLOCK)
        dist_c = t[:, None] - cmp_end[None, :]
        s_c = jnp.einsum('bghqd,bgnd->bghqn', qb, kc) * scale + head_bias(dist_c)
        p_c = masked_softmax(s_c, dist_c >= 0)
        o_c = jnp.einsum('bghqn,bgnd->bghqd', p_c.astype(vc.dtype), vc)
        imp = jnp.einsum('bghqn,nj->bgqj', p_c, overlap)
        cur = (t // SEL_BLOCK)[:, None]
        j = jnp.arange(nb)[None, :]
        forced = (j == 0) | (j == cur) | (j == cur - 1)
        imp = jnp.where(j * SEL_BLOCK <= t[:, None], imp + FORCE * forced, NEG)
        idx = lax.top_k(imp, n_sel)[1]
        kg = gather_blocks(ks, idx)
        vg = gather_blocks(vs, idx)
        dist_s = t[:, None, None] - (idx[..., None] * SEL_BLOCK + jnp.arange(SEL_BLOCK))
        bias_s = bias_flat[head_ids[None, :, :, None, None, None] * NUM_BUCKETS
                           + t5_bucket(dist_s)[:, :, None]]
        s_s = jnp.einsum('bghqd,bgqnkd->bghqnk', qb, kg) * scale + bias_s
        valid_s = (dist_s >= 0).reshape(B, N_KV, 1, Q_BLOCK, n_sel * SEL_BLOCK)
        p_s = masked_softmax(s_s.reshape(B, N_KV, HPG, Q_BLOCK, n_sel * SEL_BLOCK), valid_s)
        p_s = p_s.reshape(B, N_KV, HPG, Q_BLOCK, n_sel, SEL_BLOCK).astype(vg.dtype)
        o_s = jnp.einsum('bghqnk,bgqnkd->bghqd', p_s, vg)
        kwb = lax.dynamic_slice_in_dim(kw, s0, WINDOW + Q_BLOCK, axis=2)
        vwb = lax.dynamic_slice_in_dim(vw, s0, WINDOW + Q_BLOCK, axis=2)
        kpos = s0 - WINDOW + jnp.arange(WINDOW + Q_BLOCK)
        dist_w = t[:, None] - kpos[None, :]
        valid_w = (dist_w >= 0) & (dist_w < WINDOW) & (kpos[None, :] >= 0)
        s_w = jnp.einsum('bghqd,bgkd->bghqk', qb, kwb) * scale + head_bias(dist_w)
        p_w = masked_softmax(s_w, valid_w)
        o_w = jnp.einsum('bghqk,bgkd->bghqd', p_w.astype(vwb.dtype), vwb)
        return gb[..., 0:1] * o_c + gb[..., 1:2] * o_s + gb[..., 2:3] * o_w

    starts = jnp.arange(nq, dtype=jnp.int32) * Q_BLOCK
    o = lax.map(attend, (qb_all, gb_all, starts))
    return o.transpose(1, 0, 4, 2, 3, 5).reshape(B, S, D_ATTN)


def setup_inputs(seed: int = 0) -> dict:
    key = jax.random.key(seed)
    ks = jax.random.split(key, 17)
    f32 = jnp.float32

    def nrm(k, shape, scale):
        return jax.random.normal(k, shape, f32) * scale

    return {
        "x": nrm(ks[0], (BATCH, SEQ, D_MODEL), 1.0),
        "w_in": nrm(ks[1], (DEPTH, D_MODEL, D_IN), D_MODEL ** -0.5),
        "conv_w": nrm(ks[2], (DEPTH, CONV_WIDTH, D_CONV), CONV_WIDTH ** -0.5),
        "cmp_pe": nrm(ks[3], (DEPTH, 2, CMP_BLOCK, HEAD_DIM), 0.1),
        "cmp_w1": nrm(ks[4], (DEPTH, 2, CMP_BLOCK * HEAD_DIM, HEAD_DIM), (CMP_BLOCK * HEAD_DIM) ** -0.5),
        "cmp_w2": nrm(ks[5], (DEPTH, 2, HEAD_DIM, HEAD_DIM), HEAD_DIM ** -0.5),
        "pool_w": nrm(ks[6], (DEPTH, N_POOL, POOL_GROUP, POOL_GROUP), POOL_GROUP ** -0.5),
        "pool_scale": 1.0 + nrm(ks[7], (DEPTH, D_POOL), 0.1),
        "w_out": nrm(ks[8], (DEPTH, D_MIX, D_MODEL), BETA * D_MIX ** -0.5),
        "ln1_g": 1.0 + nrm(ks[9], (DEPTH, D_MODEL), 0.05),
        "ln1_b": nrm(ks[10], (DEPTH, D_MODEL), 0.02),
        "w_gate": nrm(ks[11], (DEPTH, D_MODEL, D_FF), D_MODEL ** -0.5),
        "w_up": nrm(ks[12], (DEPTH, D_MODEL, D_FF), D_MODEL ** -0.5),
        "w_down": nrm(ks[13], (DEPTH, D_FF, D_MODEL), BETA * D_FF ** -0.5),
        "ln2_g": 1.0 + nrm(ks[14], (DEPTH, D_MODEL), 0.05),
        "ln2_b": nrm(ks[15], (DEPTH, D_MODEL), 0.02),
        "rel_bias": nrm(ks[16], (NUM_BUCKETS, N_HEADS), 0.5),
    }


def reference(x, w_in, conv_w, cmp_pe, cmp_w1, cmp_w2, pool_w, pool_scale, w_out,
              ln1_g, ln1_b, w_gate, w_up, w_down, ln2_g, ln2_b, rel_bias):
    for l in range(DEPTH):
        h = jnp.einsum('bsd,de->bse', x, w_in[l])
        (b_gate, c_gate, x_conv, q, k_cmp, v_cmp, k_slc, v_slc, k_win, v_win,
         gate_logits, x_pool) = jnp.split(h, SPLIT_POINTS, axis=-1)
        y_a = short_conv_mixer(b_gate, c_gate, x_conv, conv_w[l])
        y_b = nsa_mixer(q, k_cmp, v_cmp, k_slc, v_slc, k_win, v_win, gate_logits,
                        cmp_pe[l], cmp_w1[l], cmp_w2[l], rel_bias)
        y_c = pool_mixer(x_pool, pool_w[l], pool_scale[l])
        mix = jnp.concatenate([y_a, y_b.astype(y_a.dtype), y_c], axis=-1)
        x = layer_norm(ALPHA * x + jnp.einsum('bsm,md->bsd', mix, w_out[l]), ln1_g[l], ln1_b[l])
        ffn = jnp.einsum('bsf,fd->bsd',
                         jax.nn.silu(jnp.einsum('bsd,df->bsf', x, w_gate[l]))
                         * jnp.einsum('bsd,df->bsf', x, w_up[l]), w_down[l])
        x = layer_norm(ALPHA * x + ffn, ln2_g[l], ln2_b[l])
    return x
```

```python
import functools
import math

import numpy as np
import jax
import jax.numpy as jnp
from jax import lax
from jax.experimental import pallas as pl
from jax.experimental.pallas import tpu as pltpu

D_MODEL = 1024
D_CONV = 256
D_POOL = 256
POOL_WINDOWS = (2, 4, 8, 16)
POOL_GROUP = 64
D_ATTN = 512
HEAD_DIM = 64
N_HEADS = 8
N_KV = 2
HPG = 4
CMP_BLOCK = 32
CMP_STRIDE = 16
SEL_BLOCK = 64
TOP_N = 8
WINDOW = 512
N_BRANCH = 3
NUM_BUCKETS = 32
MAX_DISTANCE = 128
D_FF = 2816
DEPTH = 2
ALPHA = (2 * DEPTH) ** 0.25
LN_EPS = 1e-5
NEG = -1e30
FORCE = 1e6

LANES = 128
QB = 64
KCH = 256
HALO = 16
VMEM_LIMIT = 56 * 1024 * 1024

MXU_DTYPE = jnp.bfloat16

C_CONV = 0
C_Q = 768
C_CMP = C_Q + N_HEADS * LANES
C_KV = C_CMP + N_KV * LANES
C_POOL = C_KV + N_KV * 2 * LANES
C_GATE = C_POOL + D_POOL
N_PROJ = C_GATE + N_KV * LANES

T_R = 0
T_FAR = 4
T_W3 = 5
T_CB = 7


def _t5_bucket_np(dist):
    n = np.maximum(dist, 0)
    max_exact = NUM_BUCKETS // 2
    nf = np.maximum(n, 1).astype(np.float64)
    val = np.log(nf / max_exact) / math.log(MAX_DISTANCE / max_exact) * (NUM_BUCKETS - max_exact)
    frac = np.abs(val - np.round(val))
    risky = (n > max_exact) & (n < MAX_DISTANCE) & (frac < 1e-6)
    assert not risky.any(), "bucket boundary too close to an integer"
    large = np.minimum(max_exact + np.floor(val + 1e-9).astype(np.int64), NUM_BUCKETS - 1)
    return np.where(n < max_exact, n, large).astype(np.int32)


def _bias_maps(seq):
    nq = seq // QB
    nc = (seq - CMP_BLOCK) // CMP_STRIDE + 1
    i = np.arange(QB)[:, None]
    lane = np.arange(LANES)[None, :]
    j = lane % QB
    blk = lane // QB

    def tile(dist, valid):
        return np.where(valid, _t5_bucket_np(dist), NUM_BUCKETS).astype(np.int32)

    tiles = []
    for db_of_blk in ((2, 1), (0, -1), (3, 2), (1, 0)):
        db = np.where(blk == 0, db_of_blk[0], db_of_blk[1])
        dist = QB * db + i - j
        tiles.append(tile(dist, dist >= 0))
    tiles.append(np.full((QB, LANES), NUM_BUCKETS - 1, np.int32))
    for db_of_blk in ((8, 7), (9, 8)):
        db = np.where(blk == 0, db_of_blk[0], db_of_blk[1])
        dist = QB * db + i - j
        tiles.append(tile(dist, (dist >= 0) & (dist < WINDOW)))
    for qb in range(nq):
        dist = QB * qb + i - (CMP_STRIDE * lane + CMP_BLOCK - 1)
        tiles.append(tile(dist, (dist >= 0) & (lane < nc)))
    return np.stack(tiles)


def _bias_table_kernel(map_ref, rb_ref, out_ref):
    g = pl.program_id(0)
    m = map_ref[0]
    for h in range(HPG):
        head = g * HPG + h
        acc = jnp.where(m == NUM_BUCKETS, NEG, 0.0).astype(jnp.float32)
        for b in range(NUM_BUCKETS):
            acc = jnp.where(m == b, rb_ref[b, head], acc)
        out_ref[0, 0, h * QB:(h + 1) * QB, :] = acc


def _bias_tables(rel_bias, seq):
    maps = jnp.asarray(_bias_maps(seq))
    nt = maps.shape[0]
    return pl.pallas_call(
        _bias_table_kernel,
        grid=(N_KV, nt),
        in_specs=[
            pl.BlockSpec((1, QB, LANES), lambda g, t: (t, 0, 0)),
            pl.BlockSpec(memory_space=pltpu.SMEM),
        ],
        out_specs=pl.BlockSpec((1, 1, HPG * QB, LANES), lambda g, t: (g, t, 0, 0)),
        out_shape=jax.ShapeDtypeStruct((N_KV, nt, HPG * QB, LANES), jnp.float32),
        name="bias_tables",
    )(maps, rel_bias)


def _proj_in_kernel(x_ref, w_ref, convw_ref, poolw_ref, pscale_ref,
                    yac_ref, q_ref, cmp_ref, kv_ref, gate_ref, halo_ref, *, tm):
    st = pl.program_id(1)
    first = st == 0
    xb = x_ref[0].astype(MXU_DTYPE)

    def proj(c0, width):
        return jnp.dot(xb, w_ref[:, c0:c0 + width], preferred_element_type=jnp.float32)

    @pl.when(first)
    def _():
        halo_ref[...] = jnp.zeros(halo_ref.shape, halo_ref.dtype)

    def with_halo(slot, cur):
        prev = halo_ref[slot]
        halo_ref[slot] = cur[tm - HALO:tm, :]
        return jnp.concatenate([prev, cur], axis=0)

    hc = proj(C_CONV, 3 * D_CONV)
    u = hc[:, D_CONV:2 * D_CONV] * hc[:, 2 * D_CONV:3 * D_CONV]
    ue = with_halo(0, u)
    conv = (convw_ref[2:3, :] * ue + convw_ref[1:2, :] * pltpu.roll(ue, 1, 0)
            + convw_ref[0:1, :] * pltpu.roll(ue, 2, 0))
    ya = hc[:, 0:D_CONV] * conv[HALO:, :]
    yac_ref[0, :, 0:D_CONV] = ya.astype(yac_ref.dtype)

    r = proj(C_POOL, D_POOL)
    re = with_halo(1, r)
    lane = lax.broadcasted_iota(jnp.int32, re.shape, 1)
    acc = re + pltpu.roll(re, 1, 0)
    for k, shift in enumerate((2, 4, 8)):
        acc = jnp.where(lane >= (k + 1) * POOL_GROUP, acc + pltpu.roll(acc, shift, 0), acc)
    sums = acc[HALO:, :]
    tpos = st * tm + lax.broadcasted_iota(jnp.int32, (tm, D_POOL), 0) + 1
    lane_t = lax.broadcasted_iota(jnp.int32, (tm, D_POOL), 1)
    win = jnp.left_shift(2, jnp.right_shift(lane_t, 6))
    cnt = jnp.minimum(tpos, win).astype(jnp.float32)
    d = sums / cnt - r
    yc = jnp.dot(d.astype(MXU_DTYPE), poolw_ref[...], preferred_element_type=jnp.float32) * pscale_ref[...]
    yac_ref[0, :, D_CONV:D_CONV + D_POOL] = yc.astype(yac_ref.dtype)

    q_ref[0] = proj(C_Q, N_HEADS * LANES).astype(q_ref.dtype)
    hcmp = proj(C_CMP, N_KV * LANES)
    for g in range(N_KV):
        cmp_ref[0, g] = hcmp[:, g * LANES:(g + 1) * LANES]
    kv_ref[0] = proj(C_KV, N_KV * 2 * LANES).astype(kv_ref.dtype)
    gate_ref[0] = jax.nn.sigmoid(proj(C_GATE, N_KV * LANES))


def _proj_in(x, w_proj, conv_w, pool_bd, pool_scale, tm):
    b, s, _ = x.shape
    kern = functools.partial(_proj_in_kernel, tm=tm)
    const = lambda shape: pl.BlockSpec(shape, lambda i, j: (0,) * len(shape))
    return pl.pallas_call(
        kern,
        grid=(b, s // tm),
        in_specs=[
            pl.BlockSpec((1, tm, D_MODEL), lambda i, j: (i, j, 0)),
            const((D_MODEL, N_PROJ)),
            const((3, D_CONV)),
            const((D_POOL, D_POOL)),
            const((1, D_POOL)),
        ],
        out_specs=[
            pl.BlockSpec((1, tm, D_CONV + D_POOL), lambda i, j: (i, j, 0)),
            pl.BlockSpec((1, tm, N_HEADS * LANES), lambda i, j: (i, j, 0)),
            pl.BlockSpec((1, N_KV, tm, LANES), lambda i, j: (i, 0, j, 0)),
            pl.BlockSpec((1, tm, N_KV * 2 * LANES), lambda i, j: (i, j, 0)),
            pl.BlockSpec((1, tm, N_KV * LANES), lambda i, j: (i, j, 0)),
        ],
        out_shape=[
            jax.ShapeDtypeStruct((b, s, D_CONV + D_POOL), MXU_DTYPE),
            jax.ShapeDtypeStruct((b, s, N_HEADS * LANES), MXU_DTYPE),
            jax.ShapeDtypeStruct((b, N_KV, s, LANES), jnp.float32),
            jax.ShapeDtypeStruct((b, s, N_KV * 2 * LANES), MXU_DTYPE),
            jax.ShapeDtypeStruct((b, s, N_KV * LANES), jnp.float32),
        ],
        scratch_shapes=[pltpu.VMEM((2, HALO, D_CONV), jnp.float32)],
        compiler_params=pltpu.CompilerParams(
            dimension_semantics=("arbitrary", "arbitrary"), vmem_limit_bytes=VMEM_LIMIT),
        name="proj_in",
    )(x, w_proj, conv_w, pool_bd, pool_scale)


def _compress_kernel(c_ref, pea_ref, peb_ref, w1a_ref, w1b_ref, w2_ref, out_ref):
    c = c_ref[0, 0]
    rows = c.shape[0]
    a = jnp.dot((c + pea_ref[...]).astype(MXU_DTYPE), w1a_ref[...], preferred_element_type=jnp.float32)
    bm = jnp.dot((c + peb_ref[...]).astype(MXU_DTYPE), w1b_ref[...], preferred_element_type=jnp.float32)
    pre = a + pltpu.roll(bm, rows - 1, 0)
    hid = jax.nn.gelu(pre)
    out = jnp.dot(hid.astype(MXU_DTYPE), w2_ref[...], preferred_element_type=jnp.float32)
    out_ref[0, 0] = out.astype(out_ref.dtype)


def _compress(cmp_rows, pea, peb, w1a, w1b, w2bd):
    b, g, rows, width = cmp_rows.shape
    const = lambda shape: pl.BlockSpec(shape, lambda i, j: (0,) * len(shape))
    return pl.pallas_call(
        _compress_kernel,
        grid=(b, g),
        in_specs=[
            pl.BlockSpec((1, 1, rows, width), lambda i, j: (i, j, 0, 0)),
            const((1, width)), const((1, width)),
            const((width, LANES)), const((width, LANES)), const((LANES, LANES)),
        ],
        out_specs=pl.BlockSpec((1, 1, rows, LANES), lambda i, j: (i, j, 0, 0)),
        out_shape=jax.ShapeDtypeStruct((b, g, rows, LANES), MXU_DTYPE),
        compiler_params=pltpu.CompilerParams(
            dimension_semantics=("arbitrary", "arbitrary"), vmem_limit_bytes=VMEM_LIMIT),
        name="compress",
    )(cmp_rows, pea, peb, w1a, w1b, w2bd)


def _dot_nt(a, b, precision=None):
    return lax.dot_general(a, b, (((1,), (1,)), ((), ())), precision=precision,
                           preferred_element_type=jnp.float32)


def _nsa_kernel(q_ref, kv_ref, kc_ref, gate_ref, tab_ref, ovt_ref, eye_ref, exp_ref,
                out_ref, kvs_ref, kvw_ref, sc_ref, mk_ref, vs_ref, *, seq, n_sel):
    nq = seq // QB
    nb = seq // SEL_BLOCK
    rj = vs_ref.shape[0]
    spad = 2 * QB
    rows = HPG * QB

    kvs_ref[0:spad, :] = jnp.zeros((spad, LANES), kvs_ref.dtype)
    kvs_ref[spad:, :] = kv_ref[0, :, 0:LANES]
    kvw_ref[0:WINDOW, :] = jnp.zeros((WINDOW, LANES), kvw_ref.dtype)
    kvw_ref[WINDOW:, :] = kv_ref[0, :, LANES:2 * LANES]
    kcvc = kc_ref[0, 0]

    far = tab_ref[0, T_FAR]
    far2 = jnp.concatenate([far, far], axis=1)

    def q_block(qb, carry):
        s0 = pl.multiple_of(qb * QB, QB)
        par = qb & 1
        qs = jnp.concatenate(
            [q_ref[0, pl.ds(s0, QB), h * LANES:(h + 1) * LANES] for h in range(HPG)], axis=0)

        sc = _dot_nt(qs, kcvc) + tab_ref[0, T_CB + qb]
        mc = jnp.max(sc, axis=1, keepdims=True)
        ec = jnp.where(sc > 0.5 * NEG, jnp.exp(sc - mc), 0.0)
        lc = jnp.sum(ec, axis=1, keepdims=True)
        pc = ec / jnp.where(lc > 0.0, lc, 1.0)
        o_c = jnp.dot(pc.astype(MXU_DTYPE), kcvc, preferred_element_type=jnp.float32)

        p4 = jnp.concatenate([pc[h * QB:(h + 1) * QB, :] for h in range(HPG)], axis=1)
        imp = _dot_nt(ovt_ref[...], p4, precision=lax.Precision.HIGHEST)
        jje = lax.broadcasted_iota(jnp.int32, (rj, QB), 0)
        jb = jje - 2
        valid = (jb >= 0) & (jb <= qb)
        forced = (jb == 0) | (jb == qb) | (jb == qb - 1)
        v = jnp.where(valid, imp + jnp.where(forced, FORCE, 0.0), NEG)
        vs_ref[...] = v
        rank = jnp.zeros((rj, QB), jnp.int32)
        for jp in range(2, 2 + nb):
            row = vs_ref[jp:jp + 1, :]
            ahead = (row > v) | ((row == v) & (jje > jp))
            rank = rank + ahead.astype(jnp.int32)
        sel_t = jnp.where(valid & (rank < n_sel), 1.0, 0.0)
        sel_t = jnp.concatenate([sel_t, jnp.zeros((LANES - rj, QB), jnp.float32)], axis=0)
        sel_q = _dot_nt(eye_ref[...], sel_t.astype(MXU_DTYPE)).astype(MXU_DTYPE)
        mk = jnp.dot(sel_q, exp_ref[...], preferred_element_type=jnp.float32)
        for c in range(mk_ref.shape[0]):
            mk_ref[c] = mk[:, c * LANES:(c + 1) * LANES]

        end_abs = LANES * ((qb + 2) // 2)
        nch = (end_abs + KCH - 1) // KCH
        endp = end_abs + spad

        def sel_scores(k, bias2):
            st = pl.multiple_of(endp - KCH * (k + 1), LANES)
            kvc = kvs_ref[pl.ds(st, KCH), :]
            s = _dot_nt(qs, kvc) + bias2
            c0 = st // LANES
            m1 = jnp.concatenate([mk_ref[c0], mk_ref[c0 + 1]], axis=1)
            m4 = jnp.concatenate([m1] * HPG, axis=0)
            return jnp.where(m4 > 0.5, s, NEG)

        r_lo = tab_ref[0, T_R + 2 * par]
        r_hi = tab_ref[0, T_R + 2 * par + 1]
        s_first = sel_scores(0, jnp.concatenate([r_lo, r_hi], axis=1))
        sc_ref[0] = s_first

        def pass1(k, mrun):
            s = sel_scores(k, far2)
            sc_ref[k] = s
            return jnp.maximum(mrun, s)

        mrun = lax.fori_loop(1, nch, pass1, s_first)
        ms = jnp.max(mrun, axis=1, keepdims=True)

        def pass2(k, c2):
            lrun, acc = c2
            st = pl.multiple_of(endp - KCH * (k + 1), LANES)
            p = jnp.exp(sc_ref[k] - ms)
            acc = acc + jnp.dot(p.astype(MXU_DTYPE), kvs_ref[pl.ds(st, KCH), :],
                                preferred_element_type=jnp.float32)
            return lrun + p, acc

        lrun, acc_s = lax.fori_loop(
            0, nch, pass2,
            (jnp.zeros((rows, KCH), jnp.float32), jnp.zeros((rows, LANES), jnp.float32)))
        o_s = acc_s / jnp.sum(lrun, axis=1, keepdims=True)

        endw = end_abs + WINDOW
        w3 = tab_ref[0, T_W3 + par]
        pieces = []
        for st_off, width, bias in ((KCH, KCH, jnp.concatenate([r_lo, r_hi], axis=1)),
                                    (2 * KCH, KCH, far2),
                                    (2 * KCH + LANES, LANES, w3)):
            st = pl.multiple_of(endw - st_off, LANES)
            kvc = kvw_ref[pl.ds(st, width), :]
            s = _dot_nt(qs, kvc) + bias
            lane = lax.broadcasted_iota(jnp.int32, (1, width), 1)
            s = jnp.where(lane >= WINDOW - st, s, NEG)
            pieces.append((s, kvc))
        mw = functools.reduce(jnp.maximum, [jnp.max(s, axis=1, keepdims=True) for s, _ in pieces])
        lw = jnp.zeros((rows, 1), jnp.float32)
        acc_w = jnp.zeros((rows, LANES), jnp.float32)
        for s, kvc in pieces:
            p = jnp.exp(s - mw)
            lw = lw + jnp.sum(p, axis=1, keepdims=True)
            acc_w = acc_w + jnp.dot(p.astype(MXU_DTYPE), kvc, preferred_element_type=jnp.float32)
        o_w = acc_w / lw

        gt = gate_ref[0, pl.ds(s0, QB), :]
        mixed = []
        for h in range(HPG):
            sl = slice(h * QB, (h + 1) * QB)
            c0 = h * N_BRANCH
            mixed.append(gt[:, c0:c0 + 1] * o_c[sl] + gt[:, c0 + 1:c0 + 2] * o_s[sl]
                         + gt[:, c0 + 2:c0 + 3] * o_w[sl])
        lane = lax.broadcasted_iota(jnp.int32, (QB, LANES), 1)
        pairs = [jnp.where(lane < HEAD_DIM, pltpu.roll(mixed[2 * p], HEAD_DIM, 1), mixed[2 * p + 1])
                 for p in range(HPG // 2)]
        out_ref[0, pl.ds(s0, QB), :] = jnp.concatenate(pairs, axis=1).astype(out_ref.dtype)
        return carry

    lax.fori_loop(0, nq, q_block, 0)


def _nsa(q, kv, kcvc, gates, tab, ovt, eye, expand):
    b, s, _ = q.shape
    nt = tab.shape[1]
    rj = ovt.shape[0]
    spad = 2 * QB
    kern = functools.partial(_nsa_kernel, seq=s, n_sel=min(TOP_N, s // SEL_BLOCK))
    const = lambda shape: pl.BlockSpec(shape, lambda g, i: (0,) * len(shape))
    return pl.pallas_call(
        kern,
        grid=(N_KV, b),
        in_specs=[
            pl.BlockSpec((1, s, HPG * LANES), lambda g, i: (i, 0, g)),
            pl.BlockSpec((1, s, 2 * LANES), lambda g, i: (i, 0, g)),
            pl.BlockSpec((1, 1, s // CMP_STRIDE, LANES), lambda g, i: (i, g, 0, 0)),
            pl.BlockSpec((1, s, LANES), lambda g, i: (i, 0, g)),
            pl.BlockSpec((1, nt, HPG * QB, LANES), lambda g, i: (g, 0, 0, 0)),
            const(ovt.shape), const(eye.shape), const(expand.shape),
        ],
        out_specs=pl.BlockSpec((1, s, HPG * HEAD_DIM), lambda g, i: (i, 0, g)),
        out_shape=jax.ShapeDtypeStruct((b, s, D_ATTN), MXU_DTYPE),
        scratch_shapes=[
            pltpu.VMEM((spad + s, LANES), MXU_DTYPE),
            pltpu.VMEM((WINDOW + s, LANES), MXU_DTYPE),
            pltpu.VMEM(((s + KCH) // KCH, HPG * QB, KCH), jnp.float32),
            pltpu.VMEM(((s + spad) // LANES, QB, LANES), jnp.float32),
            pltpu.VMEM((rj, QB), jnp.float32),
        ],
        compiler_params=pltpu.CompilerParams(
            dimension_semantics=("arbitrary", "arbitrary"), vmem_limit_bytes=VMEM_LIMIT),
        name="nsa",
    )(q, kv, kcvc, gates, tab, ovt, eye, expand)


def _layer_norm(x, g, b):
    mu = jnp.mean(x, axis=-1, keepdims=True)
    xc = x - mu
    var = jnp.mean(xc * xc, axis=-1, keepdims=True)
    return xc * lax.rsqrt(var + LN_EPS) * g + b


def _trunk_kernel(x_ref, yac_ref, yb_ref, woac_ref, wob_ref, g1_ref, b1_ref,
                  wg_ref, wu_ref, wd_ref, g2_ref, b2_ref, out_ref, *, fchunk):
    mix = (jnp.dot(yac_ref[...], woac_ref[...], preferred_element_type=jnp.float32)
           + jnp.dot(yb_ref[...], wob_ref[...], preferred_element_type=jnp.float32))
    x1 = _layer_norm(ALPHA * x_ref[...] + mix, g1_ref[...], b1_ref[...])
    x1b = x1.astype(MXU_DTYPE)
    ffn = jnp.zeros(x1.shape, jnp.float32)
    for c in range(D_FF // fchunk):
        cs = slice(c * fchunk, (c + 1) * fchunk)
        gate = jnp.dot(x1b, wg_ref[:, cs], preferred_element_type=jnp.float32)
        up = jnp.dot(x1b, wu_ref[:, cs], preferred_element_type=jnp.float32)
        act = (jax.nn.silu(gate) * up).astype(MXU_DTYPE)
        ffn = ffn + jnp.dot(act, wd_ref[cs, :], preferred_element_type=jnp.float32)
    out_ref[...] = _layer_norm(ALPHA * x1 + ffn, g2_ref[...], b2_ref[...])


def _trunk(x2, yac2, yb2, woac, wob, g1, b1, wg, wu, wd, g2, b2, tm, fchunk):
    n = x2.shape[0]
    kern = functools.partial(_trunk_kernel, fchunk=fchunk)
    const = lambda shape: pl.BlockSpec(shape, lambda i: (0,) * len(shape), pipeline_mode=pl.Buffered(1))
    rows = lambda width: pl.BlockSpec((tm, width), lambda i: (i, 0))
    return pl.pallas_call(
        kern,
        grid=(n // tm,),
        in_specs=[
            rows(D_MODEL), rows(D_CONV + D_POOL), rows(D_ATTN),
            const(woac.shape), const(wob.shape), const(g1.shape), const(b1.shape),
            const(wg.shape), const(wu.shape), const(wd.shape), const(g2.shape), const(b2.shape),
        ],
        out_specs=rows(D_MODEL),
        out_shape=jax.ShapeDtypeStruct((n, D_MODEL), jnp.float32),
        compiler_params=pltpu.CompilerParams(
            dimension_semantics=("arbitrary",), vmem_limit_bytes=VMEM_LIMIT),
        name="trunk",
    )(x2, yac2, yb2, woac, wob, g1, b1, wg, wu, wd, g2, b2)


def _proj_columns():
    src = np.full((N_PROJ,), -1, np.int64)
    scale = np.ones((N_PROJ,), np.float32)
    src[0:768] = np.arange(768)
    q0, kc0, vc0, ks0, vs0, kw0, vw0, gt0, xp0 = 768, 1280, 1408, 1536, 1664, 1792, 1920, 2048, 2072
    for h in range(N_HEADS):
        src[C_Q + h * LANES:C_Q + h * LANES + HEAD_DIM] = q0 + h * HEAD_DIM + np.arange(HEAD_DIM)
        scale[C_Q + h * LANES:C_Q + h * LANES + HEAD_DIM] = HEAD_DIM ** -0.5
    d = np.arange(HEAD_DIM)
    for g in range(N_KV):
        base = C_CMP + g * LANES
        src[base:base + HEAD_DIM] = kc0 + g * HEAD_DIM + d
        src[base + HEAD_DIM:base + LANES] = vc0 + g * HEAD_DIM + d
        base = C_KV + g * 2 * LANES
        for k, col in enumerate((ks0, vs0, kw0, vw0)):
            src[base + k * HEAD_DIM:base + (k + 1) * HEAD_DIM] = col + g * HEAD_DIM + d
        ng = HPG * N_BRANCH
        src[C_GATE + g * LANES:C_GATE + g * LANES + ng] = gt0 + g * ng + np.arange(ng)
    src[C_POOL:C_POOL + D_POOL] = xp0 + np.arange(D_POOL)
    return src, scale


def _block_diag(blocks):
    n = len(blocks)
    r, c = blocks[0].shape
    out = jnp.zeros((n * r, n * c), blocks[0].dtype)
    for i, blk in enumerate(blocks):
        out = out.at[i * r:(i + 1) * r, i * c:(i + 1) * c].set(blk)
    return out


def _selection_constants(seq):
    nb = seq // SEL_BLOCK
    nc = (seq - CMP_BLOCK) // CMP_STRIDE + 1
    rj = -(-(2 + nb) // 8) * 8
    assert rj <= LANES
    cmp_start = np.arange(nc) * CMP_STRIDE
    sel_start = np.arange(nb) * SEL_BLOCK
    ov = np.clip(np.minimum(cmp_start[:, None] + CMP_BLOCK, sel_start[None, :] + SEL_BLOCK)
                 - np.maximum(cmp_start[:, None], sel_start[None, :]), 0, None) / CMP_STRIDE
    ovt = np.zeros((rj, HPG * LANES), np.float32)
    for h in range(HPG):
        ovt[2:2 + nb, h * LANES:h * LANES + nc] = ov.T
    eye = np.eye(QB, dtype=np.float32)
    spad = 2 * QB
    pos = np.arange(seq + spad)
    expand = (pos[None, :] // SEL_BLOCK == np.arange(LANES)[:, None]).astype(np.float32)
    return jnp.asarray(ovt), jnp.asarray(eye, MXU_DTYPE), jnp.asarray(expand, MXU_DTYPE)


def kernel(x, w_in, conv_w, cmp_pe, cmp_w1, cmp_w2, pool_w, pool_scale, w_out,
           ln1_g, ln1_b, w_gate, w_up, w_down, ln2_g, ln2_b, rel_bias):
    b, s, _ = x.shape
    assert s // CMP_STRIDE == LANES, "compressed keys are laid out as one 128-lane tile"
    tm_proj = 512
    tm_trunk = 512
    fchunk = 256
    src, scale = _proj_columns()
    src_j = jnp.asarray(np.maximum(src, 0))
    keep = jnp.asarray((src >= 0).astype(np.float32) * scale)
    ovt, eye, expand = _selection_constants(s)
    tab = _bias_tables(rel_bias, s)
    cdt = MXU_DTYPE

    for l in range(DEPTH):
        w_proj = (w_in[l][:, src_j] * keep[None, :]).astype(cdt)
        pool_bd = _block_diag([pool_w[l, g] for g in range(len(POOL_WINDOWS))]).astype(cdt)
        yac, q, cmp_in, kv, gates = _proj_in(x, w_proj, conv_w[l], pool_bd, pool_scale[l][None, :], tm_proj)

        w1 = cmp_w1[l].reshape(2, CMP_BLOCK, HEAD_DIM, HEAD_DIM)
        w1bd = jnp.zeros((CMP_BLOCK, LANES, LANES), jnp.float32)
        w1bd = w1bd.at[:, :HEAD_DIM, :HEAD_DIM].set(w1[0]).at[:, HEAD_DIM:, HEAD_DIM:].set(w1[1])
        w1bd = w1bd.reshape(CMP_BLOCK * LANES, LANES).astype(cdt)
        half = CMP_STRIDE * LANES
        pe = jnp.concatenate([cmp_pe[l, 0], cmp_pe[l, 1]], axis=1).reshape(1, CMP_BLOCK * LANES)
        w2bd = _block_diag([cmp_w2[l, 0], cmp_w2[l, 1]]).astype(cdt)
        cmp_rows = cmp_in.reshape(b, N_KV, s // CMP_STRIDE, half)
        kcvc = _compress(cmp_rows, pe[:, :half], pe[:, half:], w1bd[:half], w1bd[half:], w2bd)

        yb = _nsa(q, kv, kcvc, gates, tab, ovt, eye, expand)

        wo = w_out[l]
        woac = jnp.concatenate([wo[0:D_CONV], wo[D_CONV + D_ATTN:]], axis=0).astype(cdt)
        wob = wo[D_CONV:D_CONV + D_ATTN].astype(cdt)
        x = _trunk(
            x.reshape(b * s, D_MODEL), yac.reshape(b * s, -1), yb.reshape(b * s, -1),
            woac, wob, ln1_g[l][None, :], ln1_b[l][None, :],
            w_gate[l].astype(cdt), w_up[l].astype(cdt), w_down[l].astype(cdt),
            ln2_g[l][None, :], ln2_b[l][None, :], tm_trunk, fchunk).reshape(b, s, D_MODEL)
    return x
```

```python
import functools
import math

import numpy as np
import jax
import jax.numpy as jnp
from jax import lax
from jax.experimental import pallas as pl
from jax.experimental.pallas import tpu as pltpu

D_MODEL = 1024
D_CONV = 256
D_POOL = 256
POOL_WINDOWS = (2, 4, 8, 16)
POOL_GROUP = 64
D_ATTN = 512
HEAD_DIM = 64
N_HEADS = 8
N_KV = 2
HPG = 4
CMP_BLOCK = 32
CMP_STRIDE = 16
SEL_BLOCK = 64
TOP_N = 8
WINDOW = 512
N_BRANCH = 3
NUM_BUCKETS = 32
MAX_DISTANCE = 128
D_FF = 2816
DEPTH = 2
ALPHA = (2 * DEPTH) ** 0.25
LN_EPS = 1e-5
NEG = -1e30
FORCE = 1e6

LANES = 128
TQ = 256
HALO = 16
VMEM_LIMIT = 56 * 1024 * 1024

MXU_DTYPE = jnp.bfloat16

C_CONV = 0
C_Q = 768
C_CMP = C_Q + N_HEADS * LANES
C_KV = C_CMP + N_KV * LANES
C_POOL = C_KV + N_KV * 2 * LANES
C_GATE = C_POOL + D_POOL
N_PROJ = C_GATE + N_KV * LANES

T_D = 0
T_A = 1
T_W = 2


def _t5_bucket_np(dist):
    n = np.maximum(dist, 0)
    max_exact = NUM_BUCKETS // 2
    nf = np.maximum(n, 1).astype(np.float64)
    val = np.log(nf / max_exact) / math.log(MAX_DISTANCE / max_exact) * (NUM_BUCKETS - max_exact)
    frac = np.abs(val - np.round(val))
    risky = (n > max_exact) & (n < MAX_DISTANCE) & (frac < 1e-6)
    assert not risky.any(), "bucket boundary too close to an integer"
    large = np.minimum(max_exact + np.floor(val + 1e-9).astype(np.int64), NUM_BUCKETS - 1)
    return np.where(n < max_exact, n, large).astype(np.int32)


def _bias_maps(seq):
    assert 2 * TQ == WINDOW
    nc = (seq - CMP_BLOCK) // CMP_STRIDE + 1
    i = np.arange(TQ)[None, :]

    def tile(dist, valid):
        return np.where(valid, _t5_bucket_np(dist), NUM_BUCKETS).astype(np.int32)

    j = np.arange(TQ)[:, None]
    att = []
    for chunks_back in range(3):
        dist = chunks_back * TQ + i - j
        att.append(tile(dist, (dist >= 0) & (dist < WINDOW)))
    n = np.arange(seq // CMP_STRIDE)[:, None]
    cmp_maps = []
    for qt in range(seq // TQ):
        dist = TQ * qt + i - (CMP_STRIDE * n + CMP_BLOCK - 1)
        cmp_maps.append(tile(dist, (dist >= 0) & (n < nc)))
    return np.stack(att), np.stack(cmp_maps)


def _bias_table_kernel(map_ref, rb_ref, out_ref, *, relative):
    g = pl.program_id(0)
    m = map_ref[0]
    for h in range(HPG):
        head = g * HPG + h
        shift = rb_ref[NUM_BUCKETS - 1, head] if relative else 0.0
        acc = jnp.where(m == NUM_BUCKETS, NEG, 0.0).astype(jnp.float32)
        for b in range(NUM_BUCKETS):
            acc = jnp.where(m == b, rb_ref[b, head] - shift, acc)
        out_ref[0, 0, :, h * TQ:(h + 1) * TQ] = acc


def _bias_table(maps, rel_bias, relative):
    nt, rows, _ = maps.shape
    return pl.pallas_call(
        functools.partial(_bias_table_kernel, relative=relative),
        grid=(N_KV, nt),
        in_specs=[
            pl.BlockSpec((1, rows, TQ), lambda g, t: (t, 0, 0)),
            pl.BlockSpec(memory_space=pltpu.SMEM),
        ],
        out_specs=pl.BlockSpec((1, 1, rows, HPG * TQ), lambda g, t: (g, t, 0, 0)),
        out_shape=jax.ShapeDtypeStruct((N_KV, nt, rows, HPG * TQ), jnp.float32),
        name="bias_table",
    )(jnp.asarray(maps), rel_bias)


def _proj_in_kernel(x_ref, w_ref, convw_ref, poolw_ref, pscale_ref,
                    yac_ref, q_ref, cmp_ref, kv_ref, gate_ref, halo_ref, *, tm):
    st = pl.program_id(1)
    first = st == 0
    xb = x_ref[0].astype(MXU_DTYPE)

    def proj(c0, width):
        return jnp.dot(xb, w_ref[:, c0:c0 + width], preferred_element_type=jnp.float32)

    @pl.when(first)
    def _():
        halo_ref[...] = jnp.zeros(halo_ref.shape, halo_ref.dtype)

    def with_halo(slot, cur):
        prev = halo_ref[slot]
        halo_ref[slot] = cur[tm - HALO:tm, :]
        return jnp.concatenate([prev, cur], axis=0)

    hc = proj(C_CONV, 3 * D_CONV)
    u = hc[:, D_CONV:2 * D_CONV] * hc[:, 2 * D_CONV:3 * D_CONV]
    ue = with_halo(0, u)
    conv = (convw_ref[2:3, :] * ue + convw_ref[1:2, :] * pltpu.roll(ue, 1, 0)
            + convw_ref[0:1, :] * pltpu.roll(ue, 2, 0))
    ya = hc[:, 0:D_CONV] * conv[HALO:, :]
    yac_ref[0, :, 0:D_CONV] = ya.astype(yac_ref.dtype)

    r = proj(C_POOL, D_POOL)
    re = with_halo(1, r)
    lane = lax.broadcasted_iota(jnp.int32, re.shape, 1)
    acc = re + pltpu.roll(re, 1, 0)
    for k, shift in enumerate((2, 4, 8)):
        acc = jnp.where(lane >= (k + 1) * POOL_GROUP, acc + pltpu.roll(acc, shift, 0), acc)
    sums = acc[HALO:, :]
    tpos = st * tm + lax.broadcasted_iota(jnp.int32, (tm, D_POOL), 0) + 1
    lane_t = lax.broadcasted_iota(jnp.int32, (tm, D_POOL), 1)
    win = jnp.left_shift(2, jnp.right_shift(lane_t, 6))
    cnt = jnp.minimum(tpos, win).astype(jnp.float32)
    d = sums / cnt - r
    yc = jnp.dot(d.astype(MXU_DTYPE), poolw_ref[...], preferred_element_type=jnp.float32) * pscale_ref[...]
    yac_ref[0, :, D_CONV:D_CONV + D_POOL] = yc.astype(yac_ref.dtype)

    q_ref[0] = proj(C_Q, N_HEADS * LANES).astype(q_ref.dtype)
    hcmp = proj(C_CMP, N_KV * LANES)
    for g in range(N_KV):
        cmp_ref[0, g] = hcmp[:, g * LANES:(g + 1) * LANES]
    kv_ref[0] = proj(C_KV, N_KV * 2 * LANES).astype(kv_ref.dtype)
    gate_ref[0] = jax.nn.sigmoid(proj(C_GATE, N_KV * LANES))


def _proj_in(x, w_proj, conv_w, pool_bd, pool_scale, tm):
    b, s, _ = x.shape
    kern = functools.partial(_proj_in_kernel, tm=tm)
    const = lambda shape: pl.BlockSpec(shape, lambda i, j: (0,) * len(shape))
    return pl.pallas_call(
        kern,
        grid=(b, s // tm),
        in_specs=[
            pl.BlockSpec((1, tm, D_MODEL), lambda i, j: (i, j, 0)),
            const((D_MODEL, N_PROJ)),
            const((3, D_CONV)),
            const((D_POOL, D_POOL)),
            const((1, D_POOL)),
        ],
        out_specs=[
            pl.BlockSpec((1, tm, D_CONV + D_POOL), lambda i, j: (i, j, 0)),
            pl.BlockSpec((1, tm, N_HEADS * LANES), lambda i, j: (i, j, 0)),
            pl.BlockSpec((1, N_KV, tm, LANES), lambda i, j: (i, 0, j, 0)),
            pl.BlockSpec((1, tm, N_KV * 2 * LANES), lambda i, j: (i, j, 0)),
            pl.BlockSpec((1, tm, N_KV * LANES), lambda i, j: (i, j, 0)),
        ],
        out_shape=[
            jax.ShapeDtypeStruct((b, s, D_CONV + D_POOL), MXU_DTYPE),
            jax.ShapeDtypeStruct((b, s, N_HEADS * LANES), MXU_DTYPE),
            jax.ShapeDtypeStruct((b, N_KV, s, LANES), jnp.float32),
            jax.ShapeDtypeStruct((b, s, N_KV * 2 * LANES), MXU_DTYPE),
            jax.ShapeDtypeStruct((b, s, N_KV * LANES), jnp.float32),
        ],
        scratch_shapes=[pltpu.VMEM((2, HALO, D_CONV), jnp.float32)],
        compiler_params=pltpu.CompilerParams(
            dimension_semantics=("arbitrary", "arbitrary"), vmem_limit_bytes=VMEM_LIMIT),
        name="proj_in",
    )(x, w_proj, conv_w, pool_bd, pool_scale)


def _compress_kernel(c_ref, pea_ref, peb_ref, w1a_ref, w1b_ref, w2_ref, out_ref):
    c = c_ref[0, 0]
    rows = c.shape[0]
    a = jnp.dot((c + pea_ref[...]).astype(MXU_DTYPE), w1a_ref[...], preferred_element_type=jnp.float32)
    bm = jnp.dot((c + peb_ref[...]).astype(MXU_DTYPE), w1b_ref[...], preferred_element_type=jnp.float32)
    pre = a + pltpu.roll(bm, rows - 1, 0)
    hid = jax.nn.gelu(pre)
    out = jnp.dot(hid.astype(MXU_DTYPE), w2_ref[...], preferred_element_type=jnp.float32)
    out_ref[0, 0] = out.astype(out_ref.dtype)


def _compress(cmp_rows, pea, peb, w1a, w1b, w2bd):
    b, g, rows, width = cmp_rows.shape
    const = lambda shape: pl.BlockSpec(shape, lambda i, j: (0,) * len(shape))
    return pl.pallas_call(
        _compress_kernel,
        grid=(b, g),
        in_specs=[
            pl.BlockSpec((1, 1, rows, width), lambda i, j: (i, j, 0, 0)),
            const((1, width)), const((1, width)),
            const((width, LANES)), const((width, LANES)), const((LANES, LANES)),
        ],
        out_specs=pl.BlockSpec((1, 1, rows, LANES), lambda i, j: (i, j, 0, 0)),
        out_shape=jax.ShapeDtypeStruct((b, g, rows, LANES), MXU_DTYPE),
        compiler_params=pltpu.CompilerParams(
            dimension_semantics=("arbitrary", "arbitrary"), vmem_limit_bytes=VMEM_LIMIT),
        name="compress",
    )(cmp_rows, pea, peb, w1a, w1b, w2bd)


def _dot_nt(a, b, precision=None):
    return lax.dot_general(a, b, (((1,), (1,)), ((), ())), precision=precision,
                           preferred_element_type=jnp.float32)


def _nsa_kernel(q_ref, kv_ref, kc_ref, gate_ref, taba_ref, tabc_ref, ovt_ref, eye_ref,
                out_ref, kas_ref, kaw_ref, vts_ref, vtw_ref, sc_ref, vs_ref, *, seq, n_sel):
    nqt = seq // TQ
    nb = seq // SEL_BLOCK
    pad_lane = HEAD_DIM + nb
    cols = HPG * TQ
    dt = kas_ref.dtype

    lane_k = lax.broadcasted_iota(jnp.int32, (seq, LANES), 1)
    row_k = lax.broadcasted_iota(jnp.int32, (seq, LANES), 0)

    def pad_rows(n):
        lane_p = lax.broadcasted_iota(jnp.int32, (n, LANES), 1)
        return jnp.where(lane_p == pad_lane, 1.0, 0.0).astype(dt)

    kvs = kv_ref[0, :, 0:LANES]
    kvw = kv_ref[0, :, LANES:2 * LANES]
    block_onehot = jnp.where(lane_k == HEAD_DIM + jnp.right_shift(row_k, 6), 1.0, 0.0).astype(dt)
    kas_ref[0:TQ, :] = pad_rows(TQ)
    kas_ref[TQ:, :] = jnp.where(lane_k < HEAD_DIM, kvs, block_onehot)
    kaw_ref[0:WINDOW, :] = pad_rows(WINDOW)
    kaw_ref[WINDOW:, :] = jnp.where(lane_k < HEAD_DIM, kvw, jnp.zeros_like(kvw))
    lane_c = lax.broadcasted_iota(jnp.int32, (TQ, LANES), 1)
    for c in range(nqt):
        for src_ref, lo, dst_ref in ((kv_ref, 0, vts_ref), (kv_ref, LANES, vtw_ref)):
            blk = src_ref[0, c * TQ:(c + 1) * TQ, lo:lo + LANES].astype(jnp.float32)
            dst_ref[c] = jnp.where(lane_c == 0, 1.0, blk).T.astype(dt)
    kcvc = kc_ref[0, 0]
    vct = kcvc.astype(jnp.float32).T.astype(dt)

    def q_tile(qt, carry):
        s0 = pl.multiple_of(qt * TQ, TQ)
        qs = jnp.concatenate(
            [q_ref[0, pl.ds(s0, TQ), h * LANES:(h + 1) * LANES] for h in range(HPG)], axis=0)
        lane_r = lax.broadcasted_iota(jnp.int32, (cols, LANES), 1)

        def pv(vt_ref, chunk, s_t, m_t):
            c = jnp.maximum(chunk, 0)
            return jnp.dot(vt_ref[c], jnp.exp(s_t - m_t).astype(dt), preferred_element_type=jnp.float32)

        def col_max(x):
            return jnp.max(x, axis=0, keepdims=True)

        sc = _dot_nt(kcvc, qs) + tabc_ref[0, qt]
        mc = col_max(sc)
        ec = jnp.where(sc > 0.5 * NEG, jnp.exp(sc - mc), 0.0)
        lc = jnp.sum(ec, axis=0, keepdims=True)
        pc = ec / jnp.where(lc > 0.0, lc, 1.0)
        o_c = jnp.dot(vct, pc.astype(dt), preferred_element_type=jnp.float32)

        q_win = jnp.where(lane_r < HEAD_DIM, qs, jnp.where(lane_r == pad_lane, NEG, 0.0).astype(dt))
        win = []
        for back in range(3):
            st = pl.multiple_of(WINDOW + (qt - back) * TQ, TQ)
            win.append(_dot_nt(kaw_ref[pl.ds(st, TQ), :], q_win) + taba_ref[0, back])
        mw = functools.reduce(jnp.maximum, [col_max(s) for s in win])
        acc_w = functools.reduce(
            lambda a, b: a + b, [pv(vtw_ref, qt - back, s, mw) for back, s in enumerate(win)])

        imp4 = jnp.dot(ovt_ref[...], pc, precision=lax.Precision.HIGHEST,
                       preferred_element_type=jnp.float32)
        imp = functools.reduce(lambda a, b: a + b, [imp4[:, h * TQ:(h + 1) * TQ] for h in range(HPG)])
        jb = lax.broadcasted_iota(jnp.int32, (nb, TQ), 0)
        qb = qt * (TQ // SEL_BLOCK) + jnp.right_shift(lax.broadcasted_iota(jnp.int32, (nb, TQ), 1), 6)
        valid = jb <= qb
        forced = (jb == 0) | (jb == qb) | (jb == qb - 1)
        v = jnp.where(valid, imp + jnp.where(forced, FORCE, 0.0), NEG)
        vs_ref[...] = v
        rank = jnp.zeros((nb, TQ), jnp.int32)
        for jp in range(nb):
            row = vs_ref[jp:jp + 1, :]
            ahead = (row > v) | ((row == v) & (jb > jp))
            rank = rank + ahead.astype(jnp.int32)
        sel_t = jnp.where(valid & (rank < n_sel), 1.0, 0.0)
        sel_t = jnp.concatenate([jnp.zeros((HEAD_DIM, TQ), jnp.float32), sel_t,
                                 jnp.zeros((LANES - HEAD_DIM - nb, TQ), jnp.float32)], axis=0)
        sel_q = _dot_nt(eye_ref[...], sel_t.astype(dt))
        u = ((sel_q - 1.0) * (-NEG)).astype(dt)
        q_sel = jnp.where(lane_r < HEAD_DIM, qs, jnp.concatenate([u] * HPG, axis=0))

        def sel_scores(chunk):
            st = pl.multiple_of((chunk + 1) * TQ, TQ)
            return _dot_nt(kas_ref[pl.ds(st, TQ), :], q_sel)

        s_d = sel_scores(qt) + taba_ref[0, T_D]
        s_a = sel_scores(qt - 1) + taba_ref[0, T_A]
        n_far = jnp.maximum(qt - 1, 0)
        n_pair = (n_far + 1) // 2

        def pass1(i, m_run):
            for k in range(2):
                c = n_far - 1 - 2 * i - k
                s = sel_scores(c)
                sc_ref[c + 1] = s
                m_run = jnp.maximum(m_run, col_max(s))
            return m_run

        ms = lax.fori_loop(0, n_pair, pass1, jnp.maximum(col_max(s_d), col_max(s_a)))

        def pass2(i, acc):
            for k in range(2):
                c = n_far - 1 - 2 * i - k
                acc = acc + pv(vts_ref, c, sc_ref[c + 1], ms)
            return acc

        acc_s = lax.fori_loop(0, n_pair, pass2, pv(vts_ref, qt, s_d, ms) + pv(vts_ref, qt - 1, s_a, ms))

        gt = gate_ref[0, pl.ds(s0, TQ), :].T
        mixed = []
        for h in range(HPG):
            cl = slice(h * TQ, (h + 1) * TQ)
            vals = slice(HEAD_DIM, 2 * HEAD_DIM)
            r0 = h * N_BRANCH
            mixed.append(gt[r0:r0 + 1] * o_c[vals, cl]
                         + (gt[r0 + 1:r0 + 2] / acc_s[0:1, cl]) * acc_s[vals, cl]
                         + (gt[r0 + 2:r0 + 3] / acc_w[0:1, cl]) * acc_w[vals, cl])
        out_ref[0, pl.ds(s0, TQ), :] = jnp.concatenate(mixed, axis=0).T.astype(out_ref.dtype)
        return carry

    lax.fori_loop(0, nqt, q_tile, 0)


def _nsa(q, kv, kcvc, gates, taba, tabc, ovt, eye):
    b, s, _ = q.shape
    nb = s // SEL_BLOCK
    nqt = s // TQ
    assert HEAD_DIM + nb < LANES and s % TQ == 0
    kern = functools.partial(_nsa_kernel, seq=s, n_sel=min(TOP_N, nb))
    const = lambda shape: pl.BlockSpec(shape, lambda g, i: (0,) * len(shape))
    return pl.pallas_call(
        kern,
        grid=(N_KV, b),
        in_specs=[
            pl.BlockSpec((1, s, HPG * LANES), lambda g, i: (i, 0, g)),
            pl.BlockSpec((1, s, 2 * LANES), lambda g, i: (i, 0, g)),
            pl.BlockSpec((1, 1, s // CMP_STRIDE, LANES), lambda g, i: (i, g, 0, 0)),
            pl.BlockSpec((1, s, LANES), lambda g, i: (i, 0, g)),
            pl.BlockSpec((1,) + taba.shape[1:], lambda g, i: (g, 0, 0, 0)),
            pl.BlockSpec((1,) + tabc.shape[1:], lambda g, i: (g, 0, 0, 0)),
            const(ovt.shape), const(eye.shape),
        ],
        out_specs=pl.BlockSpec((1, s, HPG * HEAD_DIM), lambda g, i: (i, 0, g)),
        out_shape=jax.ShapeDtypeStruct((b, s, D_ATTN), MXU_DTYPE),
        scratch_shapes=[
            pltpu.VMEM((TQ + s, LANES), MXU_DTYPE),
            pltpu.VMEM((WINDOW + s, LANES), MXU_DTYPE),
            pltpu.VMEM((nqt, LANES, TQ), MXU_DTYPE),
            pltpu.VMEM((nqt, LANES, TQ), MXU_DTYPE),
            pltpu.VMEM((nqt, TQ, HPG * TQ), jnp.float32),
            pltpu.VMEM((nb, TQ), jnp.float32),
        ],
        compiler_params=pltpu.CompilerParams(
            dimension_semantics=("arbitrary", "arbitrary"), vmem_limit_bytes=VMEM_LIMIT),
        name="nsa",
    )(q, kv, kcvc, gates, taba, tabc, ovt, eye)


def _layer_norm(x, g, b):
    mu = jnp.mean(x, axis=-1, keepdims=True)
    xc = x - mu
    var = jnp.mean(xc * xc, axis=-1, keepdims=True)
    return xc * lax.rsqrt(var + LN_EPS) * g + b


def _trunk_kernel(x_ref, yac_ref, yb_ref, woac_ref, wob_ref, g1_ref, b1_ref,
                  wg_ref, wu_ref, wd_ref, g2_ref, b2_ref, out_ref, *, fchunk):
    mix = (jnp.dot(yac_ref[...], woac_ref[...], preferred_element_type=jnp.float32)
           + jnp.dot(yb_ref[...], wob_ref[...], preferred_element_type=jnp.float32))
    x1 = _layer_norm(ALPHA * x_ref[...] + mix, g1_ref[...], b1_ref[...])
    x1b = x1.astype(MXU_DTYPE)
    ffn = jnp.zeros(x1.shape, jnp.float32)
    for c in range(D_FF // fchunk):
        cs = slice(c * fchunk, (c + 1) * fchunk)
        gate = jnp.dot(x1b, wg_ref[:, cs], preferred_element_type=jnp.float32)
        up = jnp.dot(x1b, wu_ref[:, cs], preferred_element_type=jnp.float32)
        act = (jax.nn.silu(gate) * up).astype(MXU_DTYPE)
        ffn = ffn + jnp.dot(act, wd_ref[cs, :], preferred_element_type=jnp.float32)
    out_ref[...] = _layer_norm(ALPHA * x1 + ffn, g2_ref[...], b2_ref[...])


def _trunk(x2, yac2, yb2, woac, wob, g1, b1, wg, wu, wd, g2, b2, tm, fchunk):
    n = x2.shape[0]
    kern = functools.partial(_trunk_kernel, fchunk=fchunk)
    const = lambda shape: pl.BlockSpec(shape, lambda i: (0,) * len(shape), pipeline_mode=pl.Buffered(1))
    rows = lambda width: pl.BlockSpec((tm, width), lambda i: (i, 0))
    return pl.pallas_call(
        kern,
        grid=(n // tm,),
        in_specs=[
            rows(D_MODEL), rows(D_CONV + D_POOL), rows(D_ATTN),
            const(woac.shape), const(wob.shape), const(g1.shape), const(b1.shape),
            const(wg.shape), const(wu.shape), const(wd.shape), const(g2.shape), const(b2.shape),
        ],
        out_specs=rows(D_MODEL),
        out_shape=jax.ShapeDtypeStruct((n, D_MODEL), jnp.float32),
        compiler_params=pltpu.CompilerParams(
            dimension_semantics=("arbitrary",), vmem_limit_bytes=VMEM_LIMIT),
        name="trunk",
    )(x2, yac2, yb2, woac, wob, g1, b1, wg, wu, wd, g2, b2)


def _proj_columns():
    src = np.full((N_PROJ,), -1, np.int64)
    scale = np.ones((N_PROJ,), np.float32)
    src[0:768] = np.arange(768)
    q0, kc0, vc0, ks0, vs0, kw0, vw0, gt0, xp0 = 768, 1280, 1408, 1536, 1664, 1792, 1920, 2048, 2072
    for h in range(N_HEADS):
        src[C_Q + h * LANES:C_Q + h * LANES + HEAD_DIM] = q0 + h * HEAD_DIM + np.arange(HEAD_DIM)
        scale[C_Q + h * LANES:C_Q + h * LANES + HEAD_DIM] = HEAD_DIM ** -0.5
    d = np.arange(HEAD_DIM)
    for g in range(N_KV):
        base = C_CMP + g * LANES
        src[base:base + HEAD_DIM] = kc0 + g * HEAD_DIM + d
        src[base + HEAD_DIM:base + LANES] = vc0 + g * HEAD_DIM + d
        base = C_KV + g * 2 * LANES
        for k, col in enumerate((ks0, vs0, kw0, vw0)):
            src[base + k * HEAD_DIM:base + (k + 1) * HEAD_DIM] = col + g * HEAD_DIM + d
        ng = HPG * N_BRANCH
        src[C_GATE + g * LANES:C_GATE + g * LANES + ng] = gt0 + g * ng + np.arange(ng)
    src[C_POOL:C_POOL + D_POOL] = xp0 + np.arange(D_POOL)
    return src, scale


def _block_diag(blocks):
    n = len(blocks)
    r, c = blocks[0].shape
    out = jnp.zeros((n * r, n * c), blocks[0].dtype)
    for i, blk in enumerate(blocks):
        out = out.at[i * r:(i + 1) * r, i * c:(i + 1) * c].set(blk)
    return out


def _selection_constants(seq):
    nb = seq // SEL_BLOCK
    nc = (seq - CMP_BLOCK) // CMP_STRIDE + 1
    cmp_start = np.arange(nc) * CMP_STRIDE
    sel_start = np.arange(nb) * SEL_BLOCK
    ov = np.clip(np.minimum(cmp_start[:, None] + CMP_BLOCK, sel_start[None, :] + SEL_BLOCK)
                 - np.maximum(cmp_start[:, None], sel_start[None, :]), 0, None) / CMP_STRIDE
    ovt = np.zeros((nb, seq // CMP_STRIDE), np.float32)
    ovt[:, :nc] = ov.T
    return jnp.asarray(ovt), jnp.asarray(np.eye(TQ, dtype=np.float32), MXU_DTYPE)


def kernel(x, w_in, conv_w, cmp_pe, cmp_w1, cmp_w2, pool_w, pool_scale, w_out,
           ln1_g, ln1_b, w_gate, w_up, w_down, ln2_g, ln2_b, rel_bias):
    b, s, _ = x.shape
    assert s // CMP_STRIDE == LANES, "compressed keys are laid out as one 128-lane tile"
    tm_proj = 512
    tm_trunk = 512
    fchunk = 256
    src, scale = _proj_columns()
    src_j = jnp.asarray(np.maximum(src, 0))
    keep = jnp.asarray((src >= 0).astype(np.float32) * scale)
    ovt, eye = _selection_constants(s)
    att_maps, cmp_maps = _bias_maps(s)
    taba = _bias_table(att_maps, rel_bias, True)
    tabc = _bias_table(cmp_maps, rel_bias, False)
    cdt = MXU_DTYPE

    for l in range(DEPTH):
        w_proj = (w_in[l][:, src_j] * keep[None, :]).astype(cdt)
        pool_bd = _block_diag([pool_w[l, g] for g in range(len(POOL_WINDOWS))]).astype(cdt)
        yac, q, cmp_in, kv, gates = _proj_in(x, w_proj, conv_w[l], pool_bd, pool_scale[l][None, :], tm_proj)

        w1 = cmp_w1[l].reshape(2, CMP_BLOCK, HEAD_DIM, HEAD_DIM)
        w1bd = jnp.zeros((CMP_BLOCK, LANES, LANES), jnp.float32)
        w1bd = w1bd.at[:, :HEAD_DIM, :HEAD_DIM].set(w1[0]).at[:, HEAD_DIM:, HEAD_DIM:].set(w1[1])
        w1bd = w1bd.reshape(CMP_BLOCK * LANES, LANES).astype(cdt)
        half = CMP_STRIDE * LANES
        pe = jnp.concatenate([cmp_pe[l, 0], cmp_pe[l, 1]], axis=1).reshape(1, CMP_BLOCK * LANES)
        w2bd = _block_diag([cmp_w2[l, 0], cmp_w2[l, 1]]).astype(cdt)
        cmp_rows = cmp_in.reshape(b, N_KV, s // CMP_STRIDE, half)
        kcvc = _compress(cmp_rows, pe[:, :half], pe[:, half:], w1bd[:half], w1bd[half:], w2bd)

        yb = _nsa(q, kv, kcvc, gates, taba, tabc, ovt, eye)

        wo = w_out[l]
        woac = jnp.concatenate([wo[0:D_CONV], wo[D_CONV + D_ATTN:]], axis=0).astype(cdt)
        wob = wo[D_CONV:D_CONV + D_ATTN].astype(cdt)
        x = _trunk(
            x.reshape(b * s, D_MODEL), yac.reshape(b * s, -1), yb.reshape(b * s, -1),
            woac, wob, ln1_g[l][None, :], ln1_b[l][None, :],
            w_gate[l].astype(cdt), w_up[l].astype(cdt), w_down[l].astype(cdt),
            ln2_g[l][None, :], ln2_b[l][None, :], tm_trunk, fchunk).reshape(b, s, D_MODEL)
    return x
```

```python
import functools
import math

import numpy as np
import jax
import jax.numpy as jnp
from jax import lax
from jax.experimental import pallas as pl
from jax.experimental.pallas import tpu as pltpu

D_MODEL = 1024
D_CONV = 256
D_POOL = 256
POOL_WINDOWS = (2, 4, 8, 16)
POOL_GROUP = 64
D_ATTN = 512
HEAD_DIM = 64
N_HEADS = 8
N_KV = 2
HPG = 4
CMP_BLOCK = 32
CMP_STRIDE = 16
SEL_BLOCK = 64
TOP_N = 8
WINDOW = 512
N_BRANCH = 3
NUM_BUCKETS = 32
MAX_DISTANCE = 128
D_FF = 2816
DEPTH = 2
ALPHA = (2 * DEPTH) ** 0.25
LN_EPS = 1e-5
NEG = -1e30
FORCE = 1e6

LANES = 128
TQ = 256
VT_ROWS = 96
HALO = 16
VMEM_LIMIT = 56 * 1024 * 1024

MXU_DTYPE = jnp.bfloat16

C_CONV = 0
C_Q = 768
C_CMP = C_Q + N_HEADS * HEAD_DIM
C_KV = C_CMP + N_KV * LANES
C_POOL = C_KV + N_KV * 2 * LANES
C_GATE = C_POOL + D_POOL
N_PROJ = C_GATE + N_KV * LANES

T_D = 0
T_A = 1
T_W = 2
T_FAR = 3
LOG2E = math.log2(math.e)


def _t5_bucket_np(dist):
    n = np.maximum(dist, 0)
    max_exact = NUM_BUCKETS // 2
    nf = np.maximum(n, 1).astype(np.float64)
    val = np.log(nf / max_exact) / math.log(MAX_DISTANCE / max_exact) * (NUM_BUCKETS - max_exact)
    frac = np.abs(val - np.round(val))
    risky = (n > max_exact) & (n < MAX_DISTANCE) & (frac < 1e-6)
    assert not risky.any(), "bucket boundary too close to an integer"
    large = np.minimum(max_exact + np.floor(val + 1e-9).astype(np.int64), NUM_BUCKETS - 1)
    return np.where(n < max_exact, n, large).astype(np.int32)


def _bias_maps(seq):
    assert 2 * TQ == WINDOW
    nc = (seq - CMP_BLOCK) // CMP_STRIDE + 1
    i = np.arange(TQ)[None, :]

    def tile(dist, valid):
        return np.where(valid, _t5_bucket_np(dist), NUM_BUCKETS).astype(np.int32)

    j = np.arange(TQ)[:, None]
    att = []
    for chunks_back in range(3):
        dist = chunks_back * TQ + i - j
        att.append(tile(dist, (dist >= 0) & (dist < WINDOW)))
    att.append(np.full((TQ, TQ), NUM_BUCKETS - 1, np.int32))
    n = np.arange(seq // CMP_STRIDE)[:, None]
    cmp_maps = []
    for qt in range(seq // TQ):
        dist = TQ * qt + i - (CMP_STRIDE * n + CMP_BLOCK - 1)
        cmp_maps.append(tile(dist, (dist >= 0) & (n < nc)))
    return np.stack(att), np.stack(cmp_maps)


def _bias_table_kernel(map_ref, rb_ref, out_ref, *, relative):
    g = pl.program_id(0)
    m = map_ref[0]
    for h in range(HPG):
        head = g * HPG + h
        shift = rb_ref[NUM_BUCKETS - 1, head] if relative else 0.0
        acc = jnp.where(m == NUM_BUCKETS, NEG, 0.0).astype(jnp.float32)
        for b in range(NUM_BUCKETS):
            acc = jnp.where(m == b, (rb_ref[b, head] - shift) * LOG2E, acc)
        out_ref[0, 0, :, h * TQ:(h + 1) * TQ] = acc


def _bias_table(maps, rel_bias, relative):
    nt, rows, _ = maps.shape
    return pl.pallas_call(
        functools.partial(_bias_table_kernel, relative=relative),
        grid=(N_KV, nt),
        in_specs=[
            pl.BlockSpec((1, rows, TQ), lambda g, t: (t, 0, 0)),
            pl.BlockSpec(memory_space=pltpu.SMEM),
        ],
        out_specs=pl.BlockSpec((1, 1, rows, HPG * TQ), lambda g, t: (g, t, 0, 0)),
        out_shape=jax.ShapeDtypeStruct((N_KV, nt, rows, HPG * TQ), jnp.float32),
        name="bias_table",
    )(jnp.asarray(maps), rel_bias)


def _proj_in_kernel(x_ref, w_ref, convw_ref, poolw_ref, pscale_ref,
                    yac_ref, q_ref, cmp_ref, kv_ref, gate_ref, halo_ref, *, tm):
    st = pl.program_id(1)
    first = st == 0
    xb = x_ref[0].astype(MXU_DTYPE)

    def proj(c0, width):
        return jnp.dot(xb, w_ref[:, c0:c0 + width], preferred_element_type=jnp.float32)

    @pl.when(first)
    def _():
        halo_ref[...] = jnp.zeros(halo_ref.shape, halo_ref.dtype)

    def with_halo(slot, cur):
        prev = halo_ref[slot]
        halo_ref[slot] = cur[tm - HALO:tm, :]
        return jnp.concatenate([prev, cur], axis=0)

    hc = proj(C_CONV, 3 * D_CONV)
    u = hc[:, D_CONV:2 * D_CONV] * hc[:, 2 * D_CONV:3 * D_CONV]
    ue = with_halo(0, u)
    conv = (convw_ref[2:3, :] * ue + convw_ref[1:2, :] * pltpu.roll(ue, 1, 0)
            + convw_ref[0:1, :] * pltpu.roll(ue, 2, 0))
    ya = hc[:, 0:D_CONV] * conv[HALO:, :]
    yac_ref[0, :, 0:D_CONV] = ya.astype(yac_ref.dtype)

    r = proj(C_POOL, D_POOL)
    re = with_halo(1, r)
    lane = lax.broadcasted_iota(jnp.int32, re.shape, 1)
    acc = re + pltpu.roll(re, 1, 0)
    for k, shift in enumerate((2, 4, 8)):
        acc = jnp.where(lane >= (k + 1) * POOL_GROUP, acc + pltpu.roll(acc, shift, 0), acc)
    sums = acc[HALO:, :]
    tpos = st * tm + lax.broadcasted_iota(jnp.int32, (tm, D_POOL), 0) + 1
    lane_t = lax.broadcasted_iota(jnp.int32, (tm, D_POOL), 1)
    win = jnp.left_shift(2, jnp.right_shift(lane_t, 6))
    cnt = jnp.minimum(tpos, win).astype(jnp.float32)
    d = sums / cnt - r
    yc = jnp.dot(d.astype(MXU_DTYPE), poolw_ref[...], preferred_element_type=jnp.float32) * pscale_ref[...]
    yac_ref[0, :, D_CONV:D_CONV + D_POOL] = yc.astype(yac_ref.dtype)

    q_ref[0] = proj(C_Q, N_HEADS * HEAD_DIM).astype(q_ref.dtype)
    hcmp = proj(C_CMP, N_KV * LANES)
    for g in range(N_KV):
        cmp_ref[0, g] = hcmp[:, g * LANES:(g + 1) * LANES]
    kv_ref[0] = proj(C_KV, N_KV * 2 * LANES).astype(kv_ref.dtype)
    gate_ref[0] = jax.nn.sigmoid(proj(C_GATE, N_KV * LANES))


def _proj_in(x, w_proj, conv_w, pool_bd, pool_scale, tm):
    b, s, _ = x.shape
    kern = functools.partial(_proj_in_kernel, tm=tm)
    const = lambda shape: pl.BlockSpec(shape, lambda i, j: (0,) * len(shape))
    return pl.pallas_call(
        kern,
        grid=(b, s // tm),
        in_specs=[
            pl.BlockSpec((1, tm, D_MODEL), lambda i, j: (i, j, 0)),
            const((D_MODEL, N_PROJ)),
            const((3, D_CONV)),
            const((D_POOL, D_POOL)),
            const((1, D_POOL)),
        ],
        out_specs=[
            pl.BlockSpec((1, tm, D_CONV + D_POOL), lambda i, j: (i, j, 0)),
            pl.BlockSpec((1, tm, N_HEADS * HEAD_DIM), lambda i, j: (i, j, 0)),
            pl.BlockSpec((1, N_KV, tm, LANES), lambda i, j: (i, 0, j, 0)),
            pl.BlockSpec((1, tm, N_KV * 2 * LANES), lambda i, j: (i, j, 0)),
            pl.BlockSpec((1, tm, N_KV * LANES), lambda i, j: (i, j, 0)),
        ],
        out_shape=[
            jax.ShapeDtypeStruct((b, s, D_CONV + D_POOL), MXU_DTYPE),
            jax.ShapeDtypeStruct((b, s, N_HEADS * HEAD_DIM), MXU_DTYPE),
            jax.ShapeDtypeStruct((b, N_KV, s, LANES), jnp.float32),
            jax.ShapeDtypeStruct((b, s, N_KV * 2 * LANES), MXU_DTYPE),
            jax.ShapeDtypeStruct((b, s, N_KV * LANES), jnp.float32),
        ],
        scratch_shapes=[pltpu.VMEM((2, HALO, D_CONV), jnp.float32)],
        compiler_params=pltpu.CompilerParams(
            dimension_semantics=("arbitrary", "arbitrary"), vmem_limit_bytes=VMEM_LIMIT),
        name="proj_in",
    )(x, w_proj, conv_w, pool_bd, pool_scale)


def _compress_kernel(c_ref, pea_ref, peb_ref, w1a_ref, w1b_ref, w2_ref, out_ref):
    c = c_ref[0, 0]
    rows = c.shape[0]
    a = jnp.dot((c + pea_ref[...]).astype(MXU_DTYPE), w1a_ref[...], preferred_element_type=jnp.float32)
    bm = jnp.dot((c + peb_ref[...]).astype(MXU_DTYPE), w1b_ref[...], preferred_element_type=jnp.float32)
    pre = a + pltpu.roll(bm, rows - 1, 0)
    hid = jax.nn.gelu(pre)
    out = jnp.dot(hid.astype(MXU_DTYPE), w2_ref[...], preferred_element_type=jnp.float32)
    out_ref[0, 0] = out.astype(out_ref.dtype)


def _compress(cmp_rows, pea, peb, w1a, w1b, w2bd):
    b, g, rows, width = cmp_rows.shape
    const = lambda shape: pl.BlockSpec(shape, lambda i, j: (0,) * len(shape))
    return pl.pallas_call(
        _compress_kernel,
        grid=(b, g),
        in_specs=[
            pl.BlockSpec((1, 1, rows, width), lambda i, j: (i, j, 0, 0)),
            const((1, width)), const((1, width)),
            const((width, LANES)), const((width, LANES)), const((LANES, LANES)),
        ],
        out_specs=pl.BlockSpec((1, 1, rows, LANES), lambda i, j: (i, j, 0, 0)),
        out_shape=jax.ShapeDtypeStruct((b, g, rows, LANES), MXU_DTYPE),
        compiler_params=pltpu.CompilerParams(
            dimension_semantics=("arbitrary", "arbitrary"), vmem_limit_bytes=VMEM_LIMIT),
        name="compress",
    )(cmp_rows, pea, peb, w1a, w1b, w2bd)


def _dot_nt(a, b, precision=None):
    return lax.dot_general(a, b, (((1,), (1,)), ((), ())), precision=precision,
                           preferred_element_type=jnp.float32)


def _nsa_kernel(q_ref, kv_ref, kc_ref, gate_ref, taba_ref, tabc_ref, ovt_ref, eye_ref,
                out_ref, ka_ref, vt_ref, sc0_ref, sc1_ref, cm0_ref, cm1_ref, m_ref, acc_ref, vs_ref,
                qa_ref, oc_ref,
                *, seq, n_sel):
    nqt = seq // TQ
    nb = seq // SEL_BLOCK
    pad_lane = HEAD_DIM + nb
    cols = HPG * TQ
    dt = ka_ref.dtype
    win_base = TQ + seq
    sc_refs = (sc0_ref, sc1_ref)
    cm_refs = (cm0_ref, cm1_ref)

    lane_k = lax.broadcasted_iota(jnp.int32, (seq, LANES), 1)
    row_k = lax.broadcasted_iota(jnp.int32, (seq, LANES), 0)

    def pad_rows(n):
        lane_p = lax.broadcasted_iota(jnp.int32, (n, LANES), 1)
        return jnp.where(lane_p == pad_lane, 1.0, 0.0).astype(dt)

    kvs = kv_ref[0, :, 0:LANES]
    kvw = kv_ref[0, :, LANES:2 * LANES]
    block_onehot = jnp.where(lane_k == HEAD_DIM + jnp.right_shift(row_k, 6), 1.0, 0.0).astype(dt)
    ka_ref[0:TQ, :] = pad_rows(TQ)
    ka_ref[TQ:win_base, :] = jnp.where(lane_k < HEAD_DIM, kvs, block_onehot)
    ka_ref[win_base:win_base + WINDOW, :] = pad_rows(WINDOW)
    ka_ref[win_base + WINDOW:, :] = jnp.where(lane_k < HEAD_DIM, kvw, jnp.zeros_like(kvw))
    ones_rows = jnp.where(lax.broadcasted_iota(jnp.int32, (VT_ROWS - HEAD_DIM, TQ), 0) == 0, 1.0, 0.0)
    for c in range(nqt):
        for lo, base in ((0, 0), (LANES, nqt)):
            blk = kv_ref[0, c * TQ:(c + 1) * TQ, lo:lo + LANES].astype(jnp.float32)
            vt_ref[base + c] = jnp.concatenate([blk.T[HEAD_DIM:, :], ones_rows], axis=0).astype(dt)
    kcvc = kc_ref[0, 0]
    vct = kcvc.astype(jnp.float32).T[HEAD_DIM:, :].astype(dt)
    ovt = ovt_ref[...].astype(dt)

    def col_max(x):
        return jnp.max(x, axis=0, keepdims=True)

    def prepare(qt):
        s0 = pl.multiple_of(qt * TQ, TQ)
        heads = []
        for p in range(HPG // 2):
            grp = q_ref[0, pl.ds(s0, TQ), p * LANES:(p + 1) * LANES].astype(jnp.float32)
            heads += [grp, pltpu.roll(grp, HEAD_DIM, 1)]
        qs = jnp.concatenate(heads, axis=0)
        lane_r = lax.broadcasted_iota(jnp.int32, (cols, LANES), 1)
        q_cmp = jnp.where(lane_r < HEAD_DIM, qs, 0.0).astype(dt)

        sc = _dot_nt(kcvc, q_cmp) + tabc_ref[0, qt]
        mc = col_max(sc)
        ec = jnp.where(sc > 0.5 * NEG, jnp.exp2(sc - mc), 0.0)
        lc = jnp.sum(ec, axis=0, keepdims=True)
        pc = (ec / jnp.where(lc > 0.0, lc, 1.0)).astype(dt)
        oc_ref[...] = jnp.dot(vct, pc, preferred_element_type=jnp.float32)

        imp4 = jnp.dot(ovt, pc, preferred_element_type=jnp.float32)
        imp = functools.reduce(lambda a, b: a + b, [imp4[:, h * TQ:(h + 1) * TQ] for h in range(HPG)])
        jb = lax.broadcasted_iota(jnp.int32, (nb, TQ), 0)
        qb = qt * (TQ // SEL_BLOCK) + jnp.right_shift(lax.broadcasted_iota(jnp.int32, (nb, TQ), 1), 6)
        valid = jb <= qb
        forced = (jb == 0) | (jb == qb) | (jb == qb - 1)
        v = jnp.where(valid, imp + jnp.where(forced, FORCE, 0.0), NEG)
        vs_ref[...] = v
        sub = lax.broadcasted_iota(jnp.int32, (8, TQ), 0)
        v_grp = [v[8 * k:8 * k + 8] for k in range(nb // 8)]
        r_grp = [jnp.zeros((8, TQ), jnp.int32) for _ in v_grp]
        for jp in range(nb):
            row = vs_ref[jp:jp + 1, :]
            for k, vk in enumerate(v_grp):
                if k > jp // 8:
                    ahead = row >= vk
                elif k < jp // 8:
                    ahead = row > vk
                else:
                    ahead = (row > vk) | ((row == vk) & (sub > jp % 8))
                r_grp[k] = r_grp[k] + ahead.astype(jnp.int32)
        rank = jnp.concatenate(r_grp, axis=0)
        sel_t = jnp.where(valid & (rank < n_sel), 1.0, 0.0)
        sel_t = jnp.concatenate([jnp.zeros((HEAD_DIM, TQ), jnp.float32), sel_t,
                                 jnp.zeros((LANES - HEAD_DIM - nb, TQ), jnp.float32)], axis=0)
        sel_q = _dot_nt(eye_ref[...], sel_t.astype(dt))
        u = (sel_q - 1.0) * (-NEG)
        qa_ref[...] = jnp.where(lane_r < HEAD_DIM, qs, jnp.concatenate([u] * HPG, axis=0)).astype(dt)

    def q_tile(qt, carry):
        s0 = pl.multiple_of(qt * TQ, TQ)

        n_items = qt + 4
        n_pairs = (n_items + 1) // 2

        def item(j):
            is_win = (j > qt) & (j < n_items)
            is_pad = j >= n_items
            back = j - (qt + 1)
            chunk = jnp.where(is_win, qt - back, qt - j)
            krow = jnp.where(is_win, win_base + WINDOW + chunk * TQ, (chunk + 1) * TQ)
            krow = jnp.where(is_pad, 0, krow)
            vt_idx = jnp.where(is_win, nqt, 0) + jnp.maximum(chunk, 0)
            vt_idx = jnp.where(is_pad, 0, vt_idx)
            bias_idx = jnp.where(is_win, back, jnp.where(j < T_W, j, T_FAR))
            bias_idx = jnp.where(is_pad, T_FAR, bias_idx)
            branch = jnp.where(is_win | is_pad, 1, 0)
            return pl.multiple_of(krow, TQ), vt_idx, bias_idx, branch

        def scores(j, slot):
            krow, _, bias_idx, _ = item(j)
            s = _dot_nt(ka_ref[pl.ds(krow, TQ), :], qa_ref[...]) + taba_ref[0, bias_idx]
            sc_refs[slot][...] = s
            cm_refs[slot][...] = jnp.broadcast_to(col_max(s), cm_refs[slot].shape)

        def accumulate(j, slot):
            _, vt_idx, _, branch = item(j)
            m_old = m_ref[branch]
            m_new = jnp.maximum(m_old, cm_refs[slot][...])
            m_ref[branch] = m_new
            p = jnp.exp2(sc_refs[slot][...] - m_new[0:1]).astype(dt)
            acc_ref[branch] = (acc_ref[branch] * jnp.exp2(m_old - m_new)[0:1]
                               + jnp.dot(vt_ref[vt_idx], p, preferred_element_type=jnp.float32))

        m_ref[...] = jnp.full(m_ref.shape, NEG, jnp.float32)
        acc_ref[...] = jnp.zeros(acc_ref.shape, jnp.float32)
        scores(0, 0)

        def pair(i, c):
            scores(2 * i + 1, 1)
            accumulate(2 * i, 0)
            scores(2 * i + 2, 0)
            accumulate(2 * i + 1, 1)
            return c

        lax.fori_loop(0, n_pairs - 1, pair, 0)
        last = 2 * (n_pairs - 1)
        scores(last + 1, 1)
        accumulate(last, 0)
        accumulate(last + 1, 1)

        gt = gate_ref[0, pl.ds(s0, TQ), :].T
        mixed = []
        for h in range(HPG):
            cl = slice(h * TQ, (h + 1) * TQ)
            r0 = h * N_BRANCH
            parts = [gt[r0:r0 + 1] * oc_ref[:, cl]]
            for br in range(2):
                parts.append((gt[r0 + 1 + br:r0 + 2 + br] / acc_ref[br, HEAD_DIM:HEAD_DIM + 1, cl])
                             * acc_ref[br, 0:HEAD_DIM, cl])
            mixed.append(parts[0] + parts[1] + parts[2])
        out_ref[0, pl.ds(s0, TQ), :] = jnp.concatenate(mixed, axis=0).T.astype(out_ref.dtype)

        prepare(jnp.minimum(qt + 1, nqt - 1))
        return carry

    prepare(0)
    lax.fori_loop(0, nqt, q_tile, 0)


def _nsa(q, kv, kcvc, gates, taba, tabc, ovt, eye):
    b, s, _ = q.shape
    nb = s // SEL_BLOCK
    nqt = s // TQ
    assert HEAD_DIM + nb < LANES and s % TQ == 0
    kern = functools.partial(_nsa_kernel, seq=s, n_sel=min(TOP_N, nb))
    const = lambda shape: pl.BlockSpec(shape, lambda g, i: (0,) * len(shape))
    return pl.pallas_call(
        kern,
        grid=(N_KV, b),
        in_specs=[
            pl.BlockSpec((1, s, HPG * HEAD_DIM), lambda g, i: (i, 0, g)),
            pl.BlockSpec((1, s, 2 * LANES), lambda g, i: (i, 0, g)),
            pl.BlockSpec((1, 1, s // CMP_STRIDE, LANES), lambda g, i: (i, g, 0, 0)),
            pl.BlockSpec((1, s, LANES), lambda g, i: (i, 0, g)),
            pl.BlockSpec((1,) + taba.shape[1:], lambda g, i: (g, 0, 0, 0)),
            pl.BlockSpec((1,) + tabc.shape[1:], lambda g, i: (g, 0, 0, 0)),
            const(ovt.shape), const(eye.shape),
        ],
        out_specs=pl.BlockSpec((1, s, HPG * HEAD_DIM), lambda g, i: (i, 0, g)),
        out_shape=jax.ShapeDtypeStruct((b, s, D_ATTN), MXU_DTYPE),
        scratch_shapes=[
            pltpu.VMEM((TQ + s + WINDOW + s, LANES), MXU_DTYPE),
            pltpu.VMEM((2 * nqt, VT_ROWS, TQ), MXU_DTYPE),
            pltpu.VMEM((TQ, HPG * TQ), jnp.float32),
            pltpu.VMEM((TQ, HPG * TQ), jnp.float32),
            pltpu.VMEM((8, HPG * TQ), jnp.float32),
            pltpu.VMEM((8, HPG * TQ), jnp.float32),
            pltpu.VMEM((2, 8, HPG * TQ), jnp.float32),
            pltpu.VMEM((2, VT_ROWS, HPG * TQ), jnp.float32),
            pltpu.VMEM((nb, TQ), jnp.float32),
            pltpu.VMEM((HPG * TQ, LANES), MXU_DTYPE),
            pltpu.VMEM((HEAD_DIM, HPG * TQ), jnp.float32),
        ],
        compiler_params=pltpu.CompilerParams(
            dimension_semantics=("arbitrary", "arbitrary"), vmem_limit_bytes=VMEM_LIMIT),
        name="nsa",
    )(q, kv, kcvc, gates, taba, tabc, ovt, eye)


def _layer_norm(x, g, b):
    mu = jnp.mean(x, axis=-1, keepdims=True)
    xc = x - mu
    var = jnp.mean(xc * xc, axis=-1, keepdims=True)
    return xc * lax.rsqrt(var + LN_EPS) * g + b


def _trunk_kernel(x_ref, yac_ref, yb_ref, woac_ref, wob_ref, g1_ref, b1_ref,
                  wg_ref, wu_ref, wd_ref, g2_ref, b2_ref, out_ref, *, fchunk):
    mix = (jnp.dot(yac_ref[...], woac_ref[...], preferred_element_type=jnp.float32)
           + jnp.dot(yb_ref[...], wob_ref[...], preferred_element_type=jnp.float32))
    x1 = _layer_norm(ALPHA * x_ref[...] + mix, g1_ref[...], b1_ref[...])
    x1b = x1.astype(MXU_DTYPE)
    ffn = jnp.zeros(x1.shape, jnp.float32)
    for c in range(D_FF // fchunk):
        cs = slice(c * fchunk, (c + 1) * fchunk)
        gate = jnp.dot(x1b, wg_ref[:, cs], preferred_element_type=jnp.float32)
        up = jnp.dot(x1b, wu_ref[:, cs], preferred_element_type=jnp.float32)
        act = (jax.nn.silu(gate) * up).astype(MXU_DTYPE)
        ffn = ffn + jnp.dot(act, wd_ref[cs, :], preferred_element_type=jnp.float32)
    out_ref[...] = _layer_norm(ALPHA * x1 + ffn, g2_ref[...], b2_ref[...])


def _trunk(x2, yac2, yb2, woac, wob, g1, b1, wg, wu, wd, g2, b2, tm, fchunk):
    n = x2.shape[0]
    kern = functools.partial(_trunk_kernel, fchunk=fchunk)
    const = lambda shape: pl.BlockSpec(shape, lambda i: (0,) * len(shape), pipeline_mode=pl.Buffered(1))
    rows = lambda width: pl.BlockSpec((tm, width), lambda i: (i, 0))
    return pl.pallas_call(
        kern,
        grid=(n // tm,),
        in_specs=[
            rows(D_MODEL), rows(D_CONV + D_POOL), rows(D_ATTN),
            const(woac.shape), const(wob.shape), const(g1.shape), const(b1.shape),
            const(wg.shape), const(wu.shape), const(wd.shape), const(g2.shape), const(b2.shape),
        ],
        out_specs=rows(D_MODEL),
        out_shape=jax.ShapeDtypeStruct((n, D_MODEL), jnp.float32),
        compiler_params=pltpu.CompilerParams(
            dimension_semantics=("arbitrary",), vmem_limit_bytes=VMEM_LIMIT),
        name="trunk",
    )(x2, yac2, yb2, woac, wob, g1, b1, wg, wu, wd, g2, b2)


def _proj_columns():
    src = np.full((N_PROJ,), -1, np.int64)
    scale = np.ones((N_PROJ,), np.float32)
    src[0:768] = np.arange(768)
    q0, kc0, vc0, ks0, vs0, kw0, vw0, gt0, xp0 = 768, 1280, 1408, 1536, 1664, 1792, 1920, 2048, 2072
    src[C_Q:C_Q + N_HEADS * HEAD_DIM] = q0 + np.arange(N_HEADS * HEAD_DIM)
    scale[C_Q:C_Q + N_HEADS * HEAD_DIM] = HEAD_DIM ** -0.5 * LOG2E
    d = np.arange(HEAD_DIM)
    for g in range(N_KV):
        base = C_CMP + g * LANES
        src[base:base + HEAD_DIM] = kc0 + g * HEAD_DIM + d
        src[base + HEAD_DIM:base + LANES] = vc0 + g * HEAD_DIM + d
        base = C_KV + g * 2 * LANES
        for k, col in enumerate((ks0, vs0, kw0, vw0)):
            src[base + k * HEAD_DIM:base + (k + 1) * HEAD_DIM] = col + g * HEAD_DIM + d
        ng = HPG * N_BRANCH
        src[C_GATE + g * LANES:C_GATE + g * LANES + ng] = gt0 + g * ng + np.arange(ng)
    src[C_POOL:C_POOL + D_POOL] = xp0 + np.arange(D_POOL)
    return src, scale


def _block_diag(blocks):
    n = len(blocks)
    r, c = blocks[0].shape
    out = jnp.zeros((n * r, n * c), blocks[0].dtype)
    for i, blk in enumerate(blocks):
        out = out.at[i * r:(i + 1) * r, i * c:(i + 1) * c].set(blk)
    return out


def _selection_constants(seq):
    nb = seq // SEL_BLOCK
    nc = (seq - CMP_BLOCK) // CMP_STRIDE + 1
    cmp_start = np.arange(nc) * CMP_STRIDE
    sel_start = np.arange(nb) * SEL_BLOCK
    ov = np.clip(np.minimum(cmp_start[:, None] + CMP_BLOCK, sel_start[None, :] + SEL_BLOCK)
                 - np.maximum(cmp_start[:, None], sel_start[None, :]), 0, None) / CMP_STRIDE
    ovt = np.zeros((nb, seq // CMP_STRIDE), np.float32)
    ovt[:, :nc] = ov.T
    return jnp.asarray(ovt), jnp.asarray(np.eye(TQ, dtype=np.float32), MXU_DTYPE)


def kernel(x, w_in, conv_w, cmp_pe, cmp_w1, cmp_w2, pool_w, pool_scale, w_out,
           ln1_g, ln1_b, w_gate, w_up, w_down, ln2_g, ln2_b, rel_bias):
    b, s, _ = x.shape
    assert s // CMP_STRIDE == LANES, "compressed keys are laid out as one 128-lane tile"
    tm_proj = 512
    tm_trunk = 512
    fchunk = 256
    src, scale = _proj_columns()
    src_j = jnp.asarray(np.maximum(src, 0))
    keep = jnp.asarray((src >= 0).astype(np.float32) * scale)
    ovt, eye = _selection_constants(s)
    att_maps, cmp_maps = _bias_maps(s)
    taba = _bias_table(att_maps, rel_bias, True)
    tabc = _bias_table(cmp_maps, rel_bias, False)
    cdt = MXU_DTYPE

    for l in range(DEPTH):
        w_proj = (w_in[l][:, src_j] * keep[None, :]).astype(cdt)
        pool_bd = _block_diag([pool_w[l, g] for g in range(len(POOL_WINDOWS))]).astype(cdt)
        yac, q, cmp_in, kv, gates = _proj_in(x, w_proj, conv_w[l], pool_bd, pool_scale[l][None, :], tm_proj)

        w1 = cmp_w1[l].reshape(2, CMP_BLOCK, HEAD_DIM, HEAD_DIM)
        w1bd = jnp.zeros((CMP_BLOCK, LANES, LANES), jnp.float32)
        w1bd = w1bd.at[:, :HEAD_DIM, :HEAD_DIM].set(w1[0]).at[:, HEAD_DIM:, HEAD_DIM:].set(w1[1])
        w1bd = w1bd.reshape(CMP_BLOCK * LANES, LANES).astype(cdt)
        half = CMP_STRIDE * LANES
        pe = jnp.concatenate([cmp_pe[l, 0], cmp_pe[l, 1]], axis=1).reshape(1, CMP_BLOCK * LANES)
        w2bd = _block_diag([cmp_w2[l, 0], cmp_w2[l, 1]]).astype(cdt)
        cmp_rows = cmp_in.reshape(b, N_KV, s // CMP_STRIDE, half)
        kcvc = _compress(cmp_rows, pe[:, :half], pe[:, half:], w1bd[:half], w1bd[half:], w2bd)

        yb = _nsa(q, kv, kcvc, gates, taba, tabc, ovt, eye)

        wo = w_out[l]
        woac = jnp.concatenate([wo[0:D_CONV], wo[D_CONV + D_ATTN:]], axis=0).astype(cdt)
        wob = wo[D_CONV:D_CONV + D_ATTN].astype(cdt)
        x = _trunk(
            x.reshape(b * s, D_MODEL), yac.reshape(b * s, -1), yb.reshape(b * s, -1),
            woac, wob, ln1_g[l][None, :], ln1_b[l][None, :],
            w_gate[l].astype(cdt), w_up[l].astype(cdt), w_down[l].astype(cdt),
            ln2_g[l][None, :], ln2_b[l][None, :], tm_trunk, fchunk).reshape(b, s, D_MODEL)
    return x
```

```python
import functools
import math

import numpy as np
import jax
import jax.numpy as jnp
from jax import lax
from jax.experimental import pallas as pl
from jax.experimental.pallas import tpu as pltpu

D_MODEL = 1024
D_CONV = 256
D_POOL = 256
POOL_WINDOWS = (2, 4, 8, 16)
POOL_GROUP = 64
D_ATTN = 512
HEAD_DIM = 64
N_HEADS = 8
N_KV = 2
HPG = 4
CMP_BLOCK = 32
CMP_STRIDE = 16
SEL_BLOCK = 64
TOP_N = 8
WINDOW = 512
N_BRANCH = 3
NUM_BUCKETS = 32
MAX_DISTANCE = 128
D_FF = 2816
DEPTH = 2
ALPHA = (2 * DEPTH) ** 0.25
LN_EPS = 1e-5
NEG = -1e30
FORCE = 1e6

LANES = 128
TQ = 256
VT_ROWS = 96
HALO = 16
VMEM_LIMIT = 56 * 1024 * 1024

MXU_DTYPE = jnp.bfloat16

C_CONV = 0
C_Q = 768
C_CMP = C_Q + N_HEADS * HEAD_DIM
C_KV = C_CMP + N_KV * LANES
C_POOL = C_KV + N_KV * 2 * LANES
C_GATE = C_POOL + D_POOL
N_PROJ = C_GATE + N_KV * LANES

T_D = 0
T_A = 1
T_W = 2
T_FAR = 3
LOG2E = math.log2(math.e)


def _t5_bucket_np(dist):
    n = np.maximum(dist, 0)
    max_exact = NUM_BUCKETS // 2
    nf = np.maximum(n, 1).astype(np.float64)
    val = np.log(nf / max_exact) / math.log(MAX_DISTANCE / max_exact) * (NUM_BUCKETS - max_exact)
    frac = np.abs(val - np.round(val))
    risky = (n > max_exact) & (n < MAX_DISTANCE) & (frac < 1e-6)
    assert not risky.any(), "bucket boundary too close to an integer"
    large = np.minimum(max_exact + np.floor(val + 1e-9).astype(np.int64), NUM_BUCKETS - 1)
    return np.where(n < max_exact, n, large).astype(np.int32)


def _bias_maps(seq):
    assert 2 * TQ == WINDOW
    nc = (seq - CMP_BLOCK) // CMP_STRIDE + 1
    i = np.arange(TQ)[None, :]

    def tile(dist, valid):
        return np.where(valid, _t5_bucket_np(dist), NUM_BUCKETS).astype(np.int32)

    j = np.arange(TQ)[:, None]
    att = []
    for chunks_back in range(3):
        dist = chunks_back * TQ + i - j
        att.append(tile(dist, (dist >= 0) & (dist < WINDOW)))
    att.append(np.full((TQ, TQ), NUM_BUCKETS - 1, np.int32))
    n = np.arange(seq // CMP_STRIDE)[:, None]
    cmp_maps = []
    for qt in range(seq // TQ):
        dist = TQ * qt + i - (CMP_STRIDE * n + CMP_BLOCK - 1)
        cmp_maps.append(tile(dist, (dist >= 0) & (n < nc)))
    return np.stack(att), np.stack(cmp_maps)


def _bias_table_kernel(map_ref, rb_ref, out_ref, *, relative):
    g = pl.program_id(0)
    m = map_ref[0]
    for h in range(HPG):
        head = g * HPG + h
        shift = rb_ref[NUM_BUCKETS - 1, head] if relative else 0.0
        acc = jnp.where(m == NUM_BUCKETS, NEG, 0.0).astype(jnp.float32)
        for b in range(NUM_BUCKETS):
            acc = jnp.where(m == b, (rb_ref[b, head] - shift) * LOG2E, acc)
        out_ref[0, 0, :, h * TQ:(h + 1) * TQ] = acc


def _bias_table(maps, rel_bias, relative):
    nt, rows, _ = maps.shape
    return pl.pallas_call(
        functools.partial(_bias_table_kernel, relative=relative),
        grid=(N_KV, nt),
        in_specs=[
            pl.BlockSpec((1, rows, TQ), lambda g, t: (t, 0, 0)),
            pl.BlockSpec(memory_space=pltpu.SMEM),
        ],
        out_specs=pl.BlockSpec((1, 1, rows, HPG * TQ), lambda g, t: (g, t, 0, 0)),
        out_shape=jax.ShapeDtypeStruct((N_KV, nt, rows, HPG * TQ), jnp.float32),
        name="bias_table",
    )(jnp.asarray(maps), rel_bias)


def _proj_in_kernel(x_ref, w_ref, convw_ref, poolw_ref, pscale_ref,
                    yac_ref, q_ref, cmp_ref, kv_ref, gate_ref, halo_ref, cmps_ref, *, tm):
    st = pl.program_id(1)
    first = st == 0
    xb = x_ref[0].astype(MXU_DTYPE)

    def proj(c0, width):
        return jnp.dot(xb, w_ref[:, c0:c0 + width], preferred_element_type=jnp.float32)

    @pl.when(first)
    def _():
        halo_ref[...] = jnp.zeros(halo_ref.shape, halo_ref.dtype)

    def with_halo(slot, cur):
        prev = halo_ref[slot]
        halo_ref[slot] = cur[tm - HALO:tm, :]
        return jnp.concatenate([prev, cur], axis=0)

    hc = proj(C_CONV, 3 * D_CONV)
    u = hc[:, D_CONV:2 * D_CONV] * hc[:, 2 * D_CONV:3 * D_CONV]
    ue = with_halo(0, u)
    conv = (convw_ref[2:3, :] * ue + convw_ref[1:2, :] * pltpu.roll(ue, 1, 0)
            + convw_ref[0:1, :] * pltpu.roll(ue, 2, 0))
    ya = hc[:, 0:D_CONV] * conv[HALO:, :]
    yac_ref[0, :, 0:D_CONV] = ya.astype(yac_ref.dtype)

    r = proj(C_POOL, D_POOL)
    re = with_halo(1, r)
    lane = lax.broadcasted_iota(jnp.int32, re.shape, 1)
    acc = re + pltpu.roll(re, 1, 0)
    for k, shift in enumerate((2, 4, 8)):
        acc = jnp.where(lane >= (k + 1) * POOL_GROUP, acc + pltpu.roll(acc, shift, 0), acc)
    sums = acc[HALO:, :]
    tpos = st * tm + lax.broadcasted_iota(jnp.int32, (tm, D_POOL), 0) + 1
    lane_t = lax.broadcasted_iota(jnp.int32, (tm, D_POOL), 1)
    win = jnp.left_shift(2, jnp.right_shift(lane_t, 6))
    cnt = jnp.minimum(tpos, win).astype(jnp.float32)
    d = sums / cnt - r
    yc = jnp.dot(d.astype(MXU_DTYPE), poolw_ref[...], preferred_element_type=jnp.float32) * pscale_ref[...]
    yac_ref[0, :, D_CONV:D_CONV + D_POOL] = yc.astype(yac_ref.dtype)

    q_ref[0] = proj(C_Q, N_HEADS * HEAD_DIM).astype(q_ref.dtype)
    hcmp = proj(C_CMP, N_KV * LANES)
    for g in range(N_KV):
        cmps_ref[g] = hcmp[:, g * LANES:(g + 1) * LANES]
        for r in range(CMP_STRIDE):
            cmp_ref[0, g, :, r * LANES:(r + 1) * LANES] = cmps_ref[g, pl.ds(r, tm // CMP_STRIDE, stride=CMP_STRIDE), :]
    kv_ref[0] = proj(C_KV, N_KV * 2 * LANES).astype(kv_ref.dtype)
    gate_ref[0] = jax.nn.sigmoid(proj(C_GATE, N_KV * LANES))


def _proj_in(x, w_proj, conv_w, pool_bd, pool_scale, tm):
    b, s, _ = x.shape
    kern = functools.partial(_proj_in_kernel, tm=tm)
    const = lambda shape: pl.BlockSpec(shape, lambda i, j: (0,) * len(shape))
    return pl.pallas_call(
        kern,
        grid=(b, s // tm),
        in_specs=[
            pl.BlockSpec((1, tm, D_MODEL), lambda i, j: (i, j, 0)),
            const((D_MODEL, N_PROJ)),
            const((3, D_CONV)),
            const((D_POOL, D_POOL)),
            const((1, D_POOL)),
        ],
        out_specs=[
            pl.BlockSpec((1, tm, D_CONV + D_POOL), lambda i, j: (i, j, 0)),
            pl.BlockSpec((1, tm, N_HEADS * HEAD_DIM), lambda i, j: (i, j, 0)),
            pl.BlockSpec((1, N_KV, tm // CMP_STRIDE, CMP_STRIDE * LANES), lambda i, j: (i, 0, j, 0)),
            pl.BlockSpec((1, tm, N_KV * 2 * LANES), lambda i, j: (i, j, 0)),
            pl.BlockSpec((1, tm, N_KV * LANES), lambda i, j: (i, j, 0)),
        ],
        out_shape=[
            jax.ShapeDtypeStruct((b, s, D_CONV + D_POOL), MXU_DTYPE),
            jax.ShapeDtypeStruct((b, s, N_HEADS * HEAD_DIM), MXU_DTYPE),
            jax.ShapeDtypeStruct((b, N_KV, s // CMP_STRIDE, CMP_STRIDE * LANES), jnp.float32),
            jax.ShapeDtypeStruct((b, s, N_KV * 2 * LANES), MXU_DTYPE),
            jax.ShapeDtypeStruct((b, s, N_KV * LANES), jnp.float32),
        ],
        scratch_shapes=[pltpu.VMEM((2, HALO, D_CONV), jnp.float32),
                        pltpu.VMEM((N_KV, tm, LANES), jnp.float32)],
        compiler_params=pltpu.CompilerParams(
            dimension_semantics=("arbitrary", "arbitrary"), vmem_limit_bytes=VMEM_LIMIT),
        name="proj_in",
    )(x, w_proj, conv_w, pool_bd, pool_scale)


def _compress_kernel(c_ref, pea_ref, peb_ref, w1a_ref, w1b_ref, w2_ref, out_ref):
    c = c_ref[0, 0]
    rows = c.shape[0]
    a = jnp.dot((c + pea_ref[...]).astype(MXU_DTYPE), w1a_ref[...], preferred_element_type=jnp.float32)
    bm = jnp.dot((c + peb_ref[...]).astype(MXU_DTYPE), w1b_ref[...], preferred_element_type=jnp.float32)
    pre = a + pltpu.roll(bm, rows - 1, 0)
    hid = jax.nn.gelu(pre)
    out = jnp.dot(hid.astype(MXU_DTYPE), w2_ref[...], preferred_element_type=jnp.float32)
    out_ref[0, 0] = out.astype(out_ref.dtype)


def _compress(cmp_rows, pea, peb, w1a, w1b, w2bd):
    b, g, rows, width = cmp_rows.shape
    const = lambda shape: pl.BlockSpec(shape, lambda i, j: (0,) * len(shape))
    return pl.pallas_call(
        _compress_kernel,
        grid=(b, g),
        in_specs=[
            pl.BlockSpec((1, 1, rows, width), lambda i, j: (i, j, 0, 0)),
            const((1, width)), const((1, width)),
            const((width, LANES)), const((width, LANES)), const((LANES, LANES)),
        ],
        out_specs=pl.BlockSpec((1, 1, rows, LANES), lambda i, j: (i, j, 0, 0)),
        out_shape=jax.ShapeDtypeStruct((b, g, rows, LANES), MXU_DTYPE),
        compiler_params=pltpu.CompilerParams(
            dimension_semantics=("arbitrary", "arbitrary"), vmem_limit_bytes=VMEM_LIMIT),
        name="compress",
    )(cmp_rows, pea, peb, w1a, w1b, w2bd)


def _dot_nt(a, b, precision=None):
    return lax.dot_general(a, b, (((1,), (1,)), ((), ())), precision=precision,
                           preferred_element_type=jnp.float32)


def _nsa_kernel(q_ref, kv_ref, kc_ref, gate_ref, taba_ref, tabc_ref, ovt_ref, eye_ref,
                out_ref, ka_ref, vt_ref, sc0_ref, sc1_ref, cm0_ref, cm1_ref, m_ref, acc_ref, vs_ref,
                qa_ref, oc_ref,
                *, seq, n_sel):
    nqt = seq // TQ
    nb = seq // SEL_BLOCK
    pad_lane = HEAD_DIM + nb
    cols = HPG * TQ
    dt = ka_ref.dtype
    win_base = TQ + seq
    sc_refs = (sc0_ref, sc1_ref)
    cm_refs = (cm0_ref, cm1_ref)

    lane_k = lax.broadcasted_iota(jnp.int32, (seq, LANES), 1)
    row_k = lax.broadcasted_iota(jnp.int32, (seq, LANES), 0)

    def pad_rows(n):
        lane_p = lax.broadcasted_iota(jnp.int32, (n, LANES), 1)
        return jnp.where(lane_p == pad_lane, 1.0, 0.0).astype(dt)

    kvs = kv_ref[0, :, 0:LANES]
    kvw = kv_ref[0, :, LANES:2 * LANES]
    block_onehot = jnp.where(lane_k == HEAD_DIM + jnp.right_shift(row_k, 6), 1.0, 0.0).astype(dt)
    ka_ref[0:TQ, :] = pad_rows(TQ)
    ka_ref[TQ:win_base, :] = jnp.where(lane_k < HEAD_DIM, kvs, block_onehot)
    ka_ref[win_base:win_base + WINDOW, :] = pad_rows(WINDOW)
    ka_ref[win_base + WINDOW:, :] = jnp.where(lane_k < HEAD_DIM, kvw, jnp.zeros_like(kvw))
    ones_rows = jnp.where(lax.broadcasted_iota(jnp.int32, (VT_ROWS - HEAD_DIM, TQ), 0) == 0, 1.0, 0.0)
    for c in range(nqt):
        for lo, base in ((0, 0), (LANES, nqt)):
            blk = kv_ref[0, c * TQ:(c + 1) * TQ, lo:lo + LANES].astype(jnp.float32)
            vt_ref[base + c] = jnp.concatenate([blk.T[HEAD_DIM:, :], ones_rows], axis=0).astype(dt)
    kcvc = kc_ref[0, 0]
    vct = kcvc.astype(jnp.float32).T[HEAD_DIM:, :].astype(dt)
    ovt = ovt_ref[...].astype(dt)

    def col_max(x):
        return jnp.max(x, axis=0, keepdims=True)

    def prepare(qt):
        s0 = pl.multiple_of(qt * TQ, TQ)
        heads = []
        for p in range(HPG // 2):
            grp = q_ref[0, pl.ds(s0, TQ), p * LANES:(p + 1) * LANES].astype(jnp.float32)
            heads += [grp, pltpu.roll(grp, HEAD_DIM, 1)]
        qs = jnp.concatenate(heads, axis=0)
        lane_r = lax.broadcasted_iota(jnp.int32, (cols, LANES), 1)
        q_cmp = jnp.where(lane_r < HEAD_DIM, qs, 0.0).astype(dt)

        sc = _dot_nt(kcvc, q_cmp) + tabc_ref[0, qt]
        mc = col_max(sc)
        ec = jnp.where(sc > 0.5 * NEG, jnp.exp2(sc - mc), 0.0)
        lc = jnp.sum(ec, axis=0, keepdims=True)
        pc = (ec / jnp.where(lc > 0.0, lc, 1.0)).astype(dt)
        oc_ref[qt & 1] = jnp.dot(vct, pc, preferred_element_type=jnp.float32)

        imp4 = jnp.dot(ovt, pc, preferred_element_type=jnp.float32)
        imp = functools.reduce(lambda a, b: a + b, [imp4[:, h * TQ:(h + 1) * TQ] for h in range(HPG)])
        jb = lax.broadcasted_iota(jnp.int32, (nb, TQ), 0)
        qb = qt * (TQ // SEL_BLOCK) + jnp.right_shift(lax.broadcasted_iota(jnp.int32, (nb, TQ), 1), 6)
        valid = jb <= qb
        forced = (jb == 0) | (jb == qb) | (jb == qb - 1)
        v = jnp.where(valid, imp + jnp.where(forced, FORCE, 0.0), NEG)
        vs_ref[...] = v
        sub = lax.broadcasted_iota(jnp.int32, (8, TQ), 0)
        v_grp = [v[8 * k:8 * k + 8] for k in range(nb // 8)]
        r_grp = [jnp.zeros((8, TQ), jnp.int32) for _ in v_grp]
        for jp in range(nb):
            row = vs_ref[jp:jp + 1, :]
            for k, vk in enumerate(v_grp):
                if k > jp // 8:
                    ahead = row >= vk
                elif k < jp // 8:
                    ahead = row > vk
                else:
                    ahead = (row > vk) | ((row == vk) & (sub > jp % 8))
                r_grp[k] = r_grp[k] + ahead.astype(jnp.int32)
        rank = jnp.concatenate(r_grp, axis=0)
        sel_t = jnp.where(valid & (rank < n_sel), 1.0, 0.0)
        sel_t = jnp.concatenate([jnp.zeros((HEAD_DIM, TQ), jnp.float32), sel_t,
                                 jnp.zeros((LANES - HEAD_DIM - nb, TQ), jnp.float32)], axis=0)
        sel_q = _dot_nt(eye_ref[...], sel_t.astype(dt))
        u = (sel_q - 1.0) * (-NEG)
        qa_ref[qt & 1] = jnp.where(lane_r < HEAD_DIM, qs, jnp.concatenate([u] * HPG, axis=0)).astype(dt)

    def item(qt, j):
        n_items = qt + 4
        is_win = (j > qt) & (j < n_items)
        is_pad = j >= n_items
        back = j - (qt + 1)
        chunk = jnp.where(is_win, qt - back, qt - j)
        krow = jnp.where(is_win, win_base + WINDOW + chunk * TQ, (chunk + 1) * TQ)
        krow = jnp.where(is_pad, 0, krow)
        vt_idx = jnp.where(is_win, nqt, 0) + jnp.maximum(chunk, 0)
        vt_idx = jnp.where(is_pad, 0, vt_idx)
        bias_idx = jnp.where(is_win, back, jnp.where(j < T_W, j, T_FAR))
        bias_idx = jnp.where(is_pad, T_FAR, bias_idx)
        branch = jnp.where(is_win | is_pad, 1, 0)
        return pl.multiple_of(krow, TQ), vt_idx, bias_idx, branch

    def scores(qt, j, slot):
        krow, _, bias_idx, _ = item(qt, j)
        s = _dot_nt(ka_ref[pl.ds(krow, TQ), :], qa_ref[qt & 1]) + taba_ref[0, bias_idx]
        sc_refs[slot][...] = s
        cm_refs[slot][...] = jnp.broadcast_to(col_max(s), cm_refs[slot].shape)

    def accumulate(qt, j, slot):
        _, vt_idx, _, branch = item(qt, j)
        m_old = m_ref[branch]
        m_new = jnp.maximum(m_old, cm_refs[slot][...])
        m_ref[branch] = m_new
        p = jnp.exp2(sc_refs[slot][...] - m_new[0:1]).astype(dt)
        acc_ref[branch] = (acc_ref[branch] * jnp.exp2(m_old - m_new)[0:1]
                           + jnp.dot(vt_ref[vt_idx], p, preferred_element_type=jnp.float32))

    def reset_state():
        m_ref[...] = jnp.full(m_ref.shape, NEG, jnp.float32)
        acc_ref[...] = jnp.zeros(acc_ref.shape, jnp.float32)

    def q_tile(qt, carry):
        s0 = pl.multiple_of(qt * TQ, TQ)
        n_pairs = (qt + 5) // 2
        reset_state()
        scores(qt, 0, 0)

        def pair(i, c):
            scores(qt, 2 * i + 1, 1)
            accumulate(qt, 2 * i, 0)
            scores(qt, 2 * i + 2, 0)
            accumulate(qt, 2 * i + 1, 1)
            return c

        lax.fori_loop(0, n_pairs - 1, pair, 0)
        last = 2 * (n_pairs - 1)
        scores(qt, last + 1, 1)
        accumulate(qt, last, 0)
        accumulate(qt, last + 1, 1)

        gt = gate_ref[0, pl.ds(s0, TQ), :].T
        o_c = oc_ref[qt & 1]
        mixed = []
        for h in range(HPG):
            cl = slice(h * TQ, (h + 1) * TQ)
            r0 = h * N_BRANCH
            parts = [gt[r0:r0 + 1] * o_c[:, cl]]
            for br in range(2):
                parts.append((gt[r0 + 1 + br:r0 + 2 + br] / acc_ref[br, HEAD_DIM:HEAD_DIM + 1, cl])
                             * acc_ref[br, 0:HEAD_DIM, cl])
            mixed.append(parts[0] + parts[1] + parts[2])
        out_ref[0, pl.ds(s0, TQ), :] = jnp.concatenate(mixed, axis=0).T.astype(out_ref.dtype)

        prepare(jnp.minimum(qt + 1, nqt - 1))
        return carry

    prepare(0)
    lax.fori_loop(0, nqt, q_tile, 0)


def _nsa(q, kv, kcvc, gates, taba, tabc, ovt, eye):
    b, s, _ = q.shape
    nb = s // SEL_BLOCK
    nqt = s // TQ
    assert HEAD_DIM + nb < LANES and s % TQ == 0
    kern = functools.partial(_nsa_kernel, seq=s, n_sel=min(TOP_N, nb))
    const = lambda shape: pl.BlockSpec(shape, lambda g, i: (0,) * len(shape))
    return pl.pallas_call(
        kern,
        grid=(N_KV, b),
        in_specs=[
            pl.BlockSpec((1, s, HPG * HEAD_DIM), lambda g, i: (i, 0, g)),
            pl.BlockSpec((1, s, 2 * LANES), lambda g, i: (i, 0, g)),
            pl.BlockSpec((1, 1, s // CMP_STRIDE, LANES), lambda g, i: (i, g, 0, 0)),
            pl.BlockSpec((1, s, LANES), lambda g, i: (i, 0, g)),
            pl.BlockSpec((1,) + taba.shape[1:], lambda g, i: (g, 0, 0, 0)),
            pl.BlockSpec((1,) + tabc.shape[1:], lambda g, i: (g, 0, 0, 0)),
            const(ovt.shape), const(eye.shape),
        ],
        out_specs=pl.BlockSpec((1, s, HPG * HEAD_DIM), lambda g, i: (i, 0, g)),
        out_shape=jax.ShapeDtypeStruct((b, s, D_ATTN), MXU_DTYPE),
        scratch_shapes=[
            pltpu.VMEM((TQ + s + WINDOW + s, LANES), MXU_DTYPE),
            pltpu.VMEM((2 * nqt, VT_ROWS, TQ), MXU_DTYPE),
            pltpu.VMEM((TQ, HPG * TQ), jnp.float32),
            pltpu.VMEM((TQ, HPG * TQ), jnp.float32),
            pltpu.VMEM((8, HPG * TQ), jnp.float32),
            pltpu.VMEM((8, HPG * TQ), jnp.float32),
            pltpu.VMEM((2, 8, HPG * TQ), jnp.float32),
            pltpu.VMEM((2, VT_ROWS, HPG * TQ), jnp.float32),
            pltpu.VMEM((nb, TQ), jnp.float32),
            pltpu.VMEM((2, HPG * TQ, LANES), MXU_DTYPE),
            pltpu.VMEM((2, HEAD_DIM, HPG * TQ), jnp.float32),
        ],
        compiler_params=pltpu.CompilerParams(
            dimension_semantics=("arbitrary", "arbitrary"), vmem_limit_bytes=VMEM_LIMIT),
        name="nsa",
    )(q, kv, kcvc, gates, taba, tabc, ovt, eye)


def _layer_norm(x, g, b):
    mu = jnp.mean(x, axis=-1, keepdims=True)
    xc = x - mu
    var = jnp.mean(xc * xc, axis=-1, keepdims=True)
    return xc * lax.rsqrt(var + LN_EPS) * g + b


def _trunk_kernel(x_ref, yac_ref, yb_ref, woac_ref, wob_ref, g1_ref, b1_ref,
                  wg_ref, wu_ref, wd_ref, g2_ref, b2_ref, out_ref, *, fchunk):
    mix = (jnp.dot(yac_ref[...], woac_ref[...], preferred_element_type=jnp.float32)
           + jnp.dot(yb_ref[...], wob_ref[...], preferred_element_type=jnp.float32))
    x1 = _layer_norm(ALPHA * x_ref[...] + mix, g1_ref[...], b1_ref[...])
    x1b = x1.astype(MXU_DTYPE)
    ffn = jnp.zeros(x1.shape, jnp.float32)
    for c in range(D_FF // fchunk):
        cs = slice(c * fchunk, (c + 1) * fchunk)
        gate = jnp.dot(x1b, wg_ref[:, cs], preferred_element_type=jnp.float32)
        up = jnp.dot(x1b, wu_ref[:, cs], preferred_element_type=jnp.float32)
        act = (jax.nn.silu(gate) * up).astype(MXU_DTYPE)
        ffn = ffn + jnp.dot(act, wd_ref[cs, :], preferred_element_type=jnp.float32)
    out_ref[...] = _layer_norm(ALPHA * x1 + ffn, g2_ref[...], b2_ref[...])


def _trunk(x2, yac2, yb2, woac, wob, g1, b1, wg, wu, wd, g2, b2, tm, fchunk):
    n = x2.shape[0]
    kern = functools.partial(_trunk_kernel, fchunk=fchunk)
    const = lambda shape: pl.BlockSpec(shape, lambda i: (0,) * len(shape), pipeline_mode=pl.Buffered(1))
    rows = lambda width: pl.BlockSpec((tm, width), lambda i: (i, 0))
    return pl.pallas_call(
        kern,
        grid=(n // tm,),
        in_specs=[
            rows(D_MODEL), rows(D_CONV + D_POOL), rows(D_ATTN),
            const(woac.shape), const(wob.shape), const(g1.shape), const(b1.shape),
            const(wg.shape), const(wu.shape), const(wd.shape), const(g2.shape), const(b2.shape),
        ],
        out_specs=rows(D_MODEL),
        out_shape=jax.ShapeDtypeStruct((n, D_MODEL), jnp.float32),
        compiler_params=pltpu.CompilerParams(
            dimension_semantics=("arbitrary",), vmem_limit_bytes=VMEM_LIMIT),
        name="trunk",
    )(x2, yac2, yb2, woac, wob, g1, b1, wg, wu, wd, g2, b2)


def _proj_columns():
    src = np.full((N_PROJ,), -1, np.int64)
    scale = np.ones((N_PROJ,), np.float32)
    src[0:768] = np.arange(768)
    q0, kc0, vc0, ks0, vs0, kw0, vw0, gt0, xp0 = 768, 1280, 1408, 1536, 1664, 1792, 1920, 2048, 2072
    src[C_Q:C_Q + N_HEADS * HEAD_DIM] = q0 + np.arange(N_HEADS * HEAD_DIM)
    scale[C_Q:C_Q + N_HEADS * HEAD_DIM] = HEAD_DIM ** -0.5 * LOG2E
    d = np.arange(HEAD_DIM)
    for g in range(N_KV):
        base = C_CMP + g * LANES
        src[base:base + HEAD_DIM] = kc0 + g * HEAD_DIM + d
        src[base + HEAD_DIM:base + LANES] = vc0 + g * HEAD_DIM + d
        base = C_KV + g * 2 * LANES
        for k, col in enumerate((ks0, vs0, kw0, vw0)):
            src[base + k * HEAD_DIM:base + (k + 1) * HEAD_DIM] = col + g * HEAD_DIM + d
        ng = HPG * N_BRANCH
        src[C_GATE + g * LANES:C_GATE + g * LANES + ng] = gt0 + g * ng + np.arange(ng)
    src[C_POOL:C_POOL + D_POOL] = xp0 + np.arange(D_POOL)
    return src, scale


def _proj_runs():
    src, scale = _proj_columns()
    runs, i = [], 0
    while i < N_PROJ:
        j = i + 1
        while j < N_PROJ and scale[j] == scale[i] and (
                (src[i] < 0 and src[j] < 0) or (src[i] >= 0 and src[j] == src[j - 1] + 1)):
            j += 1
        runs.append((int(src[i]), j - i, float(scale[i])))
        i = j
    return runs


def _block_diag(blocks):
    n = len(blocks)
    r, c = blocks[0].shape
    out = jnp.zeros((n * r, n * c), blocks[0].dtype)
    for i, blk in enumerate(blocks):
        out = out.at[i * r:(i + 1) * r, i * c:(i + 1) * c].set(blk)
    return out


def _selection_constants(seq):
    nb = seq // SEL_BLOCK
    nc = (seq - CMP_BLOCK) // CMP_STRIDE + 1
    cmp_start = np.arange(nc) * CMP_STRIDE
    sel_start = np.arange(nb) * SEL_BLOCK
    ov = np.clip(np.minimum(cmp_start[:, None] + CMP_BLOCK, sel_start[None, :] + SEL_BLOCK)
                 - np.maximum(cmp_start[:, None], sel_start[None, :]), 0, None) / CMP_STRIDE
    ovt = np.zeros((nb, seq // CMP_STRIDE), np.float32)
    ovt[:, :nc] = ov.T
    return jnp.asarray(ovt), jnp.asarray(np.eye(TQ, dtype=np.float32), MXU_DTYPE)


def kernel(x, w_in, conv_w, cmp_pe, cmp_w1, cmp_w2, pool_w, pool_scale, w_out,
           ln1_g, ln1_b, w_gate, w_up, w_down, ln2_g, ln2_b, rel_bias):
    b, s, _ = x.shape
    assert s // CMP_STRIDE == LANES, "compressed keys are laid out as one 128-lane tile"
    tm_proj = 512
    tm_trunk = 512
    fchunk = 256
    runs = _proj_runs()
    ovt, eye = _selection_constants(s)
    att_maps, cmp_maps = _bias_maps(s)
    taba = _bias_table(att_maps, rel_bias, True)
    tabc = _bias_table(cmp_maps, rel_bias, False)
    cdt = MXU_DTYPE

    for l in range(DEPTH):
        w_proj = jnp.concatenate(
            [jnp.zeros((D_MODEL, n), cdt) if a < 0 else (w_in[l][:, a:a + n] * sc).astype(cdt)
             for a, n, sc in runs], axis=1)
        pool_bd = _block_diag([pool_w[l, g] for g in range(len(POOL_WINDOWS))]).astype(cdt)
        yac, q, cmp_rows, kv, gates = _proj_in(x, w_proj, conv_w[l], pool_bd, pool_scale[l][None, :], tm_proj)

        w1 = cmp_w1[l].reshape(2, CMP_BLOCK, HEAD_DIM, HEAD_DIM)
        w1bd = jnp.zeros((CMP_BLOCK, LANES, LANES), jnp.float32)
        w1bd = w1bd.at[:, :HEAD_DIM, :HEAD_DIM].set(w1[0]).at[:, HEAD_DIM:, HEAD_DIM:].set(w1[1])
        w1bd = w1bd.reshape(CMP_BLOCK * LANES, LANES).astype(cdt)
        half = CMP_STRIDE * LANES
        pe = jnp.concatenate([cmp_pe[l, 0], cmp_pe[l, 1]], axis=1).reshape(1, CMP_BLOCK * LANES)
        w2bd = _block_diag([cmp_w2[l, 0], cmp_w2[l, 1]]).astype(cdt)
        kcvc = _compress(cmp_rows, pe[:, :half], pe[:, half:], w1bd[:half], w1bd[half:], w2bd)

        yb = _nsa(q, kv, kcvc, gates, taba, tabc, ovt, eye)

        wo = w_out[l]
        woac = jnp.concatenate([wo[0:D_CONV], wo[D_CONV + D_ATTN:]], axis=0).astype(cdt)
        wob = wo[D_CONV:D_CONV + D_ATTN].astype(cdt)
        x = _trunk(
            x.reshape(b * s, D_MODEL), yac.reshape(b * s, -1), yb.reshape(b * s, -1),
            woac, wob, ln1_g[l][None, :], ln1_b[l][None, :],
            w_gate[l].astype(cdt), w_up[l].astype(cdt), w_down[l].astype(cdt),
            ln2_g[l][None, :], ln2_b[l][None, :], tm_trunk, fchunk).reshape(b, s, D_MODEL)
    return x
```

```python
import functools
import math

import numpy as np
import jax
import jax.numpy as jnp
from jax import lax
from jax.experimental import pallas as pl
from jax.experimental.pallas import tpu as pltpu

D_MODEL = 1024
D_CONV = 256
D_POOL = 256
POOL_WINDOWS = (2, 4, 8, 16)
POOL_GROUP = 64
D_ATTN = 512
HEAD_DIM = 64
N_HEADS = 8
N_KV = 2
HPG = 4
CMP_BLOCK = 32
CMP_STRIDE = 16
SEL_BLOCK = 64
TOP_N = 8
WINDOW = 512
N_BRANCH = 3
NUM_BUCKETS = 32
MAX_DISTANCE = 128
D_FF = 2816
DEPTH = 2
ALPHA = (2 * DEPTH) ** 0.25
LN_EPS = 1e-5
NEG = -1e30
FORCE = 1e6

LANES = 128
TQ = 256
VT_ROWS = 96
HALO = 16
VMEM_LIMIT = 56 * 1024 * 1024

MXU_DTYPE = jnp.bfloat16

C_CONV = 0
C_Q = 768
C_CMP = C_Q + N_HEADS * HEAD_DIM
C_KV = C_CMP + N_KV * LANES
C_POOL = C_KV + N_KV * 2 * LANES
C_GATE = C_POOL + D_POOL
N_PROJ = C_GATE + N_KV * LANES

T_D = 0
T_A = 1
T_W = 2
T_FAR = 3
LOG2E = math.log2(math.e)


def _t5_bucket_np(dist):
    n = np.maximum(dist, 0)
    max_exact = NUM_BUCKETS // 2
    nf = np.maximum(n, 1).astype(np.float64)
    val = np.log(nf / max_exact) / math.log(MAX_DISTANCE / max_exact) * (NUM_BUCKETS - max_exact)
    frac = np.abs(val - np.round(val))
    risky = (n > max_exact) & (n < MAX_DISTANCE) & (frac < 1e-6)
    assert not risky.any(), "bucket boundary too close to an integer"
    large = np.minimum(max_exact + np.floor(val + 1e-9).astype(np.int64), NUM_BUCKETS - 1)
    return np.where(n < max_exact, n, large).astype(np.int32)


def _bias_maps(seq):
    assert 2 * TQ == WINDOW
    nc = (seq - CMP_BLOCK) // CMP_STRIDE + 1
    i = np.arange(TQ)[None, :]

    def tile(dist, valid):
        return np.where(valid, _t5_bucket_np(dist), NUM_BUCKETS).astype(np.int32)

    j = np.arange(TQ)[:, None]
    att = []
    for chunks_back in range(3):
        dist = chunks_back * TQ + i - j
        att.append(tile(dist, (dist >= 0) & (dist < WINDOW)))
    att.append(np.full((TQ, TQ), NUM_BUCKETS - 1, np.int32))
    n = np.arange(seq // CMP_STRIDE)[:, None]
    cmp_maps = []
    for qt in range(seq // TQ):
        dist = TQ * qt + i - (CMP_STRIDE * n + CMP_BLOCK - 1)
        cmp_maps.append(tile(dist, (dist >= 0) & (n < nc)))
    return np.stack(att), np.stack(cmp_maps)


def _bias_table_kernel(map_ref, rb_ref, out_ref, *, relative):
    g = pl.program_id(0)
    m = map_ref[0]
    for h in range(HPG):
        head = g * HPG + h
        shift = rb_ref[NUM_BUCKETS - 1, head] if relative else 0.0
        acc = jnp.where(m == NUM_BUCKETS, NEG, 0.0).astype(jnp.float32)
        for b in range(NUM_BUCKETS):
            acc = jnp.where(m == b, (rb_ref[b, head] - shift) * LOG2E, acc)
        out_ref[0, 0, :, h * TQ:(h + 1) * TQ] = acc


def _bias_table(maps, rel_bias, relative):
    nt, rows, _ = maps.shape
    return pl.pallas_call(
        functools.partial(_bias_table_kernel, relative=relative),
        grid=(N_KV, nt),
        in_specs=[
            pl.BlockSpec((1, rows, TQ), lambda g, t: (t, 0, 0)),
            pl.BlockSpec(memory_space=pltpu.SMEM),
        ],
        out_specs=pl.BlockSpec((1, 1, rows, HPG * TQ), lambda g, t: (g, t, 0, 0)),
        out_shape=jax.ShapeDtypeStruct((N_KV, nt, rows, HPG * TQ), jnp.float32),
        name="bias_table",
    )(jnp.asarray(maps), rel_bias)


def _proj_in_kernel(x_ref, w_ref, convw_ref, poolw_ref, pscale_ref,
                    yac_ref, q_ref, cmp_ref, kv_ref, gate_ref, halo_ref, cmps_ref, *, tm):
    st = pl.program_id(1)
    first = st == 0
    xb = x_ref[0].astype(MXU_DTYPE)

    def proj(c0, width):
        return jnp.dot(xb, w_ref[:, c0:c0 + width], preferred_element_type=jnp.float32)

    @pl.when(first)
    def _():
        halo_ref[...] = jnp.zeros(halo_ref.shape, halo_ref.dtype)

    def with_halo(slot, cur):
        prev = halo_ref[slot]
        halo_ref[slot] = cur[tm - HALO:tm, :]
        return jnp.concatenate([prev, cur], axis=0)

    hc = proj(C_CONV, 3 * D_CONV)
    u = hc[:, D_CONV:2 * D_CONV] * hc[:, 2 * D_CONV:3 * D_CONV]
    ue = with_halo(0, u)
    conv = (convw_ref[2:3, :] * ue + convw_ref[1:2, :] * pltpu.roll(ue, 1, 0)
            + convw_ref[0:1, :] * pltpu.roll(ue, 2, 0))
    ya = hc[:, 0:D_CONV] * conv[HALO:, :]
    yac_ref[0, :, 0:D_CONV] = ya.astype(yac_ref.dtype)

    r = proj(C_POOL, D_POOL)
    re = with_halo(1, r)
    lane = lax.broadcasted_iota(jnp.int32, re.shape, 1)
    acc = re + pltpu.roll(re, 1, 0)
    for k, shift in enumerate((2, 4, 8)):
        acc = jnp.where(lane >= (k + 1) * POOL_GROUP, acc + pltpu.roll(acc, shift, 0), acc)
    sums = acc[HALO:, :]
    tpos = st * tm + lax.broadcasted_iota(jnp.int32, (tm, D_POOL), 0) + 1
    lane_t = lax.broadcasted_iota(jnp.int32, (tm, D_POOL), 1)
    win = jnp.left_shift(2, jnp.right_shift(lane_t, 6))
    cnt = jnp.minimum(tpos, win).astype(jnp.float32)
    d = sums / cnt - r
    yc = jnp.dot(d.astype(MXU_DTYPE), poolw_ref[...], preferred_element_type=jnp.float32) * pscale_ref[...]
    yac_ref[0, :, D_CONV:D_CONV + D_POOL] = yc.astype(yac_ref.dtype)

    q_ref[0] = proj(C_Q, N_HEADS * HEAD_DIM).astype(q_ref.dtype)
    hcmp = proj(C_CMP, N_KV * LANES)
    for g in range(N_KV):
        cmps_ref[g] = hcmp[:, g * LANES:(g + 1) * LANES]
        for r in range(CMP_STRIDE):
            cmp_ref[0, g, :, r * LANES:(r + 1) * LANES] = cmps_ref[g, pl.ds(r, tm // CMP_STRIDE, stride=CMP_STRIDE), :]
    kv_ref[0] = proj(C_KV, N_KV * 2 * LANES).astype(kv_ref.dtype)
    gate_ref[0] = jax.nn.sigmoid(proj(C_GATE, N_KV * LANES))


def _proj_in(x, w_proj, conv_w, pool_bd, pool_scale, tm):
    b, s, _ = x.shape
    kern = functools.partial(_proj_in_kernel, tm=tm)
    const = lambda shape: pl.BlockSpec(shape, lambda i, j: (0,) * len(shape))
    return pl.pallas_call(
        kern,
        grid=(b, s // tm),
        in_specs=[
            pl.BlockSpec((1, tm, D_MODEL), lambda i, j: (i, j, 0)),
            const((D_MODEL, N_PROJ)),
            const((3, D_CONV)),
            const((D_POOL, D_POOL)),
            const((1, D_POOL)),
        ],
        out_specs=[
            pl.BlockSpec((1, tm, D_CONV + D_POOL), lambda i, j: (i, j, 0)),
            pl.BlockSpec((1, tm, N_HEADS * HEAD_DIM), lambda i, j: (i, j, 0)),
            pl.BlockSpec((1, N_KV, tm // CMP_STRIDE, CMP_STRIDE * LANES), lambda i, j: (i, 0, j, 0)),
            pl.BlockSpec((1, tm, N_KV * 2 * LANES), lambda i, j: (i, j, 0)),
            pl.BlockSpec((1, tm, N_KV * LANES), lambda i, j: (i, j, 0)),
        ],
        out_shape=[
            jax.ShapeDtypeStruct((b, s, D_CONV + D_POOL), MXU_DTYPE),
            jax.ShapeDtypeStruct((b, s, N_HEADS * HEAD_DIM), MXU_DTYPE),
            jax.ShapeDtypeStruct((b, N_KV, s // CMP_STRIDE, CMP_STRIDE * LANES), jnp.float32),
            jax.ShapeDtypeStruct((b, s, N_KV * 2 * LANES), MXU_DTYPE),
            jax.ShapeDtypeStruct((b, s, N_KV * LANES), jnp.float32),
        ],
        scratch_shapes=[pltpu.VMEM((2, HALO, D_CONV), jnp.float32),
                        pltpu.VMEM((N_KV, tm, LANES), jnp.float32)],
        compiler_params=pltpu.CompilerParams(
            dimension_semantics=("arbitrary", "arbitrary"), vmem_limit_bytes=VMEM_LIMIT),
        name="proj_in",
    )(x, w_proj, conv_w, pool_bd, pool_scale)


def _compress_kernel(c_ref, pea_ref, peb_ref, w1a_ref, w1b_ref, w2_ref, out_ref):
    c = c_ref[0, 0]
    rows = c.shape[0]
    a = jnp.dot((c + pea_ref[...]).astype(MXU_DTYPE), w1a_ref[...], preferred_element_type=jnp.float32)
    bm = jnp.dot((c + peb_ref[...]).astype(MXU_DTYPE), w1b_ref[...], preferred_element_type=jnp.float32)
    pre = a + pltpu.roll(bm, rows - 1, 0)
    hid = jax.nn.gelu(pre)
    out = jnp.dot(hid.astype(MXU_DTYPE), w2_ref[...], preferred_element_type=jnp.float32)
    out_ref[0, 0] = out.astype(out_ref.dtype)


def _compress(cmp_rows, pea, peb, w1a, w1b, w2bd):
    b, g, rows, width = cmp_rows.shape
    const = lambda shape: pl.BlockSpec(shape, lambda i, j: (0,) * len(shape))
    return pl.pallas_call(
        _compress_kernel,
        grid=(b, g),
        in_specs=[
            pl.BlockSpec((1, 1, rows, width), lambda i, j: (i, j, 0, 0)),
            const((1, width)), const((1, width)),
            const((width, LANES)), const((width, LANES)), const((LANES, LANES)),
        ],
        out_specs=pl.BlockSpec((1, 1, rows, LANES), lambda i, j: (i, j, 0, 0)),
        out_shape=jax.ShapeDtypeStruct((b, g, rows, LANES), MXU_DTYPE),
        compiler_params=pltpu.CompilerParams(
            dimension_semantics=("arbitrary", "arbitrary"), vmem_limit_bytes=VMEM_LIMIT),
        name="compress",
    )(cmp_rows, pea, peb, w1a, w1b, w2bd)


def _dot_nt(a, b, precision=None):
    return lax.dot_general(a, b, (((1,), (1,)), ((), ())), precision=precision,
                           preferred_element_type=jnp.float32)


def _nsa_kernel(q_ref, kv_ref, kc_ref, gate_ref, taba_ref, tabc_ref, ovt_ref, eye_ref,
                out_ref, ka_ref, vt_ref, sc0_ref, sc1_ref, cm0_ref, cm1_ref, m_ref, acc_ref, vs_ref,
                qa_ref, oc_ref,
                *, seq, n_sel):
    nqt = seq // TQ
    nb = seq // SEL_BLOCK
    pad_lane = HEAD_DIM + nb
    cols = HPG * TQ
    dt = ka_ref.dtype
    win_base = TQ + seq
    sc_refs = (sc0_ref, sc1_ref)
    cm_refs = (cm0_ref, cm1_ref)

    lane_k = lax.broadcasted_iota(jnp.int32, (seq, LANES), 1)
    row_k = lax.broadcasted_iota(jnp.int32, (seq, LANES), 0)

    def pad_rows(n):
        lane_p = lax.broadcasted_iota(jnp.int32, (n, LANES), 1)
        return jnp.where(lane_p == pad_lane, 1.0, 0.0).astype(dt)

    kvs = kv_ref[0, :, 0:LANES]
    kvw = kv_ref[0, :, LANES:2 * LANES]
    block_onehot = jnp.where(lane_k == HEAD_DIM + jnp.right_shift(row_k, 6), 1.0, 0.0).astype(dt)
    ka_ref[0:TQ, :] = pad_rows(TQ)
    ka_ref[TQ:win_base, :] = jnp.where(lane_k < HEAD_DIM, kvs, block_onehot)
    ka_ref[win_base:win_base + WINDOW, :] = pad_rows(WINDOW)
    ka_ref[win_base + WINDOW:, :] = jnp.where(lane_k < HEAD_DIM, kvw, jnp.zeros_like(kvw))
    ones_rows = jnp.where(lax.broadcasted_iota(jnp.int32, (VT_ROWS - HEAD_DIM, TQ), 0) == 0, 1.0, 0.0)
    for c in range(nqt):
        for lo, base in ((0, 0), (LANES, nqt)):
            blk = kv_ref[0, c * TQ:(c + 1) * TQ, lo:lo + LANES].astype(jnp.float32)
            vt_ref[base + c] = jnp.concatenate([blk.T[HEAD_DIM:, :], ones_rows], axis=0).astype(dt)
    kcvc = kc_ref[0, 0]
    vct = kcvc.astype(jnp.float32).T[HEAD_DIM:, :].astype(dt)
    ovt = ovt_ref[...].astype(dt)

    def col_max(x):
        return jnp.max(x, axis=0, keepdims=True)

    def prepare(qt):
        s0 = pl.multiple_of(qt * TQ, TQ)
        heads = []
        for p in range(HPG // 2):
            grp = q_ref[0, pl.ds(s0, TQ), p * LANES:(p + 1) * LANES].astype(jnp.float32)
            heads += [grp, pltpu.roll(grp, HEAD_DIM, 1)]
        qs = jnp.concatenate(heads, axis=0)
        lane_r = lax.broadcasted_iota(jnp.int32, (cols, LANES), 1)
        q_cmp = jnp.where(lane_r < HEAD_DIM, qs, 0.0).astype(dt)

        sc = _dot_nt(kcvc, q_cmp) + tabc_ref[0, qt]
        yield
        mc = col_max(sc)
        ec = jnp.where(sc > 0.5 * NEG, jnp.exp2(sc - mc), 0.0)
        lc = jnp.sum(ec, axis=0, keepdims=True)
        pc = (ec / jnp.where(lc > 0.0, lc, 1.0)).astype(dt)
        oc_ref[qt & 1] = jnp.dot(vct, pc, preferred_element_type=jnp.float32)

        imp4 = jnp.dot(ovt, pc, preferred_element_type=jnp.float32)
        yield
        imp = functools.reduce(lambda a, b: a + b, [imp4[:, h * TQ:(h + 1) * TQ] for h in range(HPG)])
        jb = lax.broadcasted_iota(jnp.int32, (nb, TQ), 0)
        qb = qt * (TQ // SEL_BLOCK) + jnp.right_shift(lax.broadcasted_iota(jnp.int32, (nb, TQ), 1), 6)
        valid = jb <= qb
        forced = (jb == 0) | (jb == qb) | (jb == qb - 1)
        v = jnp.where(valid, imp + jnp.where(forced, FORCE, 0.0), NEG)
        vs_ref[...] = v
        sub = lax.broadcasted_iota(jnp.int32, (8, TQ), 0)
        v_grp = [v[8 * k:8 * k + 8] for k in range(nb // 8)]
        r_grp = [jnp.zeros((8, TQ), jnp.int32) for _ in v_grp]
        for jp in range(nb):
            row = vs_ref[jp:jp + 1, :]
            for k, vk in enumerate(v_grp):
                if k > jp // 8:
                    ahead = row >= vk
                elif k < jp // 8:
                    ahead = row > vk
                else:
                    ahead = (row > vk) | ((row == vk) & (sub > jp % 8))
                r_grp[k] = r_grp[k] + ahead.astype(jnp.int32)
        rank = jnp.concatenate(r_grp, axis=0)
        sel_t = jnp.where(valid & (rank < n_sel), 1.0, 0.0)
        sel_t = jnp.concatenate([jnp.zeros((HEAD_DIM, TQ), jnp.float32), sel_t,
                                 jnp.zeros((LANES - HEAD_DIM - nb, TQ), jnp.float32)], axis=0)
        sel_q = _dot_nt(eye_ref[...], sel_t.astype(dt))
        u = (sel_q - 1.0) * (-NEG)
        qa_ref[qt & 1] = jnp.where(lane_r < HEAD_DIM, qs, jnp.concatenate([u] * HPG, axis=0)).astype(dt)

    def item(qt, j):
        n_items = qt + 4
        is_win = (j > qt) & (j < n_items)
        is_pad = j >= n_items
        back = j - (qt + 1)
        chunk = jnp.where(is_win, qt - back, qt - j)
        krow = jnp.where(is_win, win_base + WINDOW + chunk * TQ, (chunk + 1) * TQ)
        krow = jnp.where(is_pad, 0, krow)
        vt_idx = jnp.where(is_win, nqt, 0) + jnp.maximum(chunk, 0)
        vt_idx = jnp.where(is_pad, 0, vt_idx)
        bias_idx = jnp.where(is_win, back, jnp.where(j < T_W, j, T_FAR))
        bias_idx = jnp.where(is_pad, T_FAR, bias_idx)
        branch = jnp.where(is_win | is_pad, 1, 0)
        return pl.multiple_of(krow, TQ), vt_idx, bias_idx, branch

    def scores(qt, j, slot):
        krow, _, bias_idx, _ = item(qt, j)
        s = _dot_nt(ka_ref[pl.ds(krow, TQ), :], qa_ref[qt & 1]) + taba_ref[0, bias_idx]
        sc_refs[slot][...] = s
        cm_refs[slot][...] = jnp.broadcast_to(col_max(s), cm_refs[slot].shape)

    def accumulate(qt, j, slot):
        _, vt_idx, _, branch = item(qt, j)
        m_old = m_ref[branch]
        m_new = jnp.maximum(m_old, cm_refs[slot][...])
        m_ref[branch] = m_new
        p = jnp.exp2(sc_refs[slot][...] - m_new[0:1]).astype(dt)
        acc_ref[branch] = (acc_ref[branch] * jnp.exp2(m_old - m_new)[0:1]
                           + jnp.dot(vt_ref[vt_idx], p, preferred_element_type=jnp.float32))

    def reset_state():
        m_ref[...] = jnp.full(m_ref.shape, NEG, jnp.float32)
        acc_ref[...] = jnp.zeros(acc_ref.shape, jnp.float32)

    def q_tile(qt, carry):
        s0 = pl.multiple_of(qt * TQ, TQ)
        n_pairs = (qt + 5) // 2
        reset_state()
        scores(qt, 0, 0)

        def pair(i):
            scores(qt, 2 * i + 1, 1)
            accumulate(qt, 2 * i, 0)
            scores(qt, 2 * i + 2, 0)
            accumulate(qt, 2 * i + 1, 1)

        def two_pairs(k, c):
            pair(2 * k)
            pair(2 * k + 1)
            return c

        n_loop = n_pairs - 1
        lax.fori_loop(0, n_loop // 2, two_pairs, 0)

        @pl.when(n_loop % 2 == 1)
        def _():
            pair(n_loop - 1)

        last = 2 * (n_pairs - 1)
        prep = prepare(jnp.minimum(qt + 1, nqt - 1))
        next(prep)
        scores(qt, last + 1, 1)
        accumulate(qt, last, 0)
        next(prep)
        accumulate(qt, last + 1, 1)
        for _ in prep:
            pass

        gt = gate_ref[0, pl.ds(s0, TQ), :].T
        o_c = oc_ref[qt & 1]
        mixed = []
        for h in range(HPG):
            cl = slice(h * TQ, (h + 1) * TQ)
            r0 = h * N_BRANCH
            parts = [gt[r0:r0 + 1] * o_c[:, cl]]
            for br in range(2):
                parts.append((gt[r0 + 1 + br:r0 + 2 + br] / acc_ref[br, HEAD_DIM:HEAD_DIM + 1, cl])
                             * acc_ref[br, 0:HEAD_DIM, cl])
            mixed.append(parts[0] + parts[1] + parts[2])
        out_ref[0, pl.ds(s0, TQ), :] = jnp.concatenate(mixed, axis=0).T.astype(out_ref.dtype)
        return carry

    for _ in prepare(0):
        pass
    lax.fori_loop(0, nqt, q_tile, 0)


def _nsa(q, kv, kcvc, gates, taba, tabc, ovt, eye):
    b, s, _ = q.shape
    nb = s // SEL_BLOCK
    nqt = s // TQ
    assert HEAD_DIM + nb < LANES and s % TQ == 0
    kern = functools.partial(_nsa_kernel, seq=s, n_sel=min(TOP_N, nb))
    const = lambda shape: pl.BlockSpec(shape, lambda g, i: (0,) * len(shape))
    return pl.pallas_call(
        kern,
        grid=(N_KV, b),
        in_specs=[
            pl.BlockSpec((1, s, HPG * HEAD_DIM), lambda g, i: (i, 0, g)),
            pl.BlockSpec((1, s, 2 * LANES), lambda g, i: (i, 0, g)),
            pl.BlockSpec((1, 1, s // CMP_STRIDE, LANES), lambda g, i: (i, g, 0, 0)),
            pl.BlockSpec((1, s, LANES), lambda g, i: (i, 0, g)),
            pl.BlockSpec((1,) + taba.shape[1:], lambda g, i: (g, 0, 0, 0)),
            pl.BlockSpec((1,) + tabc.shape[1:], lambda g, i: (g, 0, 0, 0)),
            const(ovt.shape), const(eye.shape),
        ],
        out_specs=pl.BlockSpec((1, s, HPG * HEAD_DIM), lambda g, i: (i, 0, g)),
        out_shape=jax.ShapeDtypeStruct((b, s, D_ATTN), MXU_DTYPE),
        scratch_shapes=[
            pltpu.VMEM((TQ + s + WINDOW + s, LANES), MXU_DTYPE),
            pltpu.VMEM((2 * nqt, VT_ROWS, TQ), MXU_DTYPE),
            pltpu.VMEM((TQ, HPG * TQ), jnp.float32),
            pltpu.VMEM((TQ, HPG * TQ), jnp.float32),
            pltpu.VMEM((8, HPG * TQ), jnp.float32),
            pltpu.VMEM((8, HPG * TQ), jnp.float32),
            pltpu.VMEM((2, 8, HPG * TQ), jnp.float32),
            pltpu.VMEM((2, VT_ROWS, HPG * TQ), jnp.float32),
            pltpu.VMEM((nb, TQ), jnp.float32),
            pltpu.VMEM((2, HPG * TQ, LANES), MXU_DTYPE),
            pltpu.VMEM((2, HEAD_DIM, HPG * TQ), jnp.float32),
        ],
        compiler_params=pltpu.CompilerParams(
            dimension_semantics=("arbitrary", "arbitrary"), vmem_limit_bytes=VMEM_LIMIT),
        name="nsa",
    )(q, kv, kcvc, gates, taba, tabc, ovt, eye)


def _layer_norm(x, g, b):
    mu = jnp.mean(x, axis=-1, keepdims=True)
    xc = x - mu
    var = jnp.mean(xc * xc, axis=-1, keepdims=True)
    return xc * lax.rsqrt(var + LN_EPS) * g + b


def _trunk_kernel(x_ref, yac_ref, yb_ref, woac_ref, wob_ref, g1_ref, b1_ref,
                  wg_ref, wu_ref, wd_ref, g2_ref, b2_ref, out_ref, *, fchunk):
    mix = (jnp.dot(yac_ref[...], woac_ref[...], preferred_element_type=jnp.float32)
           + jnp.dot(yb_ref[...], wob_ref[...], preferred_element_type=jnp.float32))
    x1 = _layer_norm(ALPHA * x_ref[...] + mix, g1_ref[...], b1_ref[...])
    x1b = x1.astype(MXU_DTYPE)
    ffn = jnp.zeros(x1.shape, jnp.float32)
    for c in range(D_FF // fchunk):
        cs = slice(c * fchunk, (c + 1) * fchunk)
        gate = jnp.dot(x1b, wg_ref[:, cs], preferred_element_type=jnp.float32)
        up = jnp.dot(x1b, wu_ref[:, cs], preferred_element_type=jnp.float32)
        act = (jax.nn.silu(gate) * up).astype(MXU_DTYPE)
        ffn = ffn + jnp.dot(act, wd_ref[cs, :], preferred_element_type=jnp.float32)
    out_ref[...] = _layer_norm(ALPHA * x1 + ffn, g2_ref[...], b2_ref[...])


def _trunk(x2, yac2, yb2, woac, wob, g1, b1, wg, wu, wd, g2, b2, tm, fchunk):
    n = x2.shape[0]
    kern = functools.partial(_trunk_kernel, fchunk=fchunk)
    const = lambda shape: pl.BlockSpec(shape, lambda i: (0,) * len(shape), pipeline_mode=pl.Buffered(1))
    rows = lambda width: pl.BlockSpec((tm, width), lambda i: (i, 0))
    return pl.pallas_call(
        kern,
        grid=(n // tm,),
        in_specs=[
            rows(D_MODEL), rows(D_CONV + D_POOL), rows(D_ATTN),
            const(woac.shape), const(wob.shape), const(g1.shape), const(b1.shape),
            const(wg.shape), const(wu.shape), const(wd.shape), const(g2.shape), const(b2.shape),
        ],
        out_specs=rows(D_MODEL),
        out_shape=jax.ShapeDtypeStruct((n, D_MODEL), jnp.float32),
        compiler_params=pltpu.CompilerParams(
            dimension_semantics=("arbitrary",), vmem_limit_bytes=VMEM_LIMIT),
        name="trunk",
    )(x2, yac2, yb2, woac, wob, g1, b1, wg, wu, wd, g2, b2)


def _proj_columns():
    src = np.full((N_PROJ,), -1, np.int64)
    scale = np.ones((N_PROJ,), np.float32)
    src[0:768] = np.arange(768)
    q0, kc0, vc0, ks0, vs0, kw0, vw0, gt0, xp0 = 768, 1280, 1408, 1536, 1664, 1792, 1920, 2048, 2072
    src[C_Q:C_Q + N_HEADS * HEAD_DIM] = q0 + np.arange(N_HEADS * HEAD_DIM)
    scale[C_Q:C_Q + N_HEADS * HEAD_DIM] = HEAD_DIM ** -0.5 * LOG2E
    d = np.arange(HEAD_DIM)
    for g in range(N_KV):
        base = C_CMP + g * LANES
        src[base:base + HEAD_DIM] = kc0 + g * HEAD_DIM + d
        src[base + HEAD_DIM:base + LANES] = vc0 + g * HEAD_DIM + d
        base = C_KV + g * 2 * LANES
        for k, col in enumerate((ks0, vs0, kw0, vw0)):
            src[base + k * HEAD_DIM:base + (k + 1) * HEAD_DIM] = col + g * HEAD_DIM + d
        ng = HPG * N_BRANCH
        src[C_GATE + g * LANES:C_GATE + g * LANES + ng] = gt0 + g * ng + np.arange(ng)
    src[C_POOL:C_POOL + D_POOL] = xp0 + np.arange(D_POOL)
    return src, scale


def _proj_runs():
    src, scale = _proj_columns()
    runs, i = [], 0
    while i < N_PROJ:
        j = i + 1
        while j < N_PROJ and scale[j] == scale[i] and (
                (src[i] < 0 and src[j] < 0) or (src[i] >= 0 and src[j] == src[j - 1] + 1)):
            j += 1
        runs.append((int(src[i]), j - i, float(scale[i])))
        i = j
    return runs


def _block_diag(blocks):
    n = len(blocks)
    r, c = blocks[0].shape
    out = jnp.zeros((n * r, n * c), blocks[0].dtype)
    for i, blk in enumerate(blocks):
        out = out.at[i * r:(i + 1) * r, i * c:(i + 1) * c].set(blk)
    return out


def _selection_constants(seq):
    nb = seq // SEL_BLOCK
    nc = (seq - CMP_BLOCK) // CMP_STRIDE + 1
    cmp_start = np.arange(nc) * CMP_STRIDE
    sel_start = np.arange(nb) * SEL_BLOCK
    ov = np.clip(np.minimum(cmp_start[:, None] + CMP_BLOCK, sel_start[None, :] + SEL_BLOCK)
                 - np.maximum(cmp_start[:, None], sel_start[None, :]), 0, None) / CMP_STRIDE
    ovt = np.zeros((nb, seq // CMP_STRIDE), np.float32)
    ovt[:, :nc] = ov.T
    return jnp.asarray(ovt), jnp.asarray(np.eye(TQ, dtype=np.float32), MXU_DTYPE)


def kernel(x, w_in, conv_w, cmp_pe, cmp_w1, cmp_w2, pool_w, pool_scale, w_out,
           ln1_g, ln1_b, w_gate, w_up, w_down, ln2_g, ln2_b, rel_bias):
    b, s, _ = x.shape
    assert s // CMP_STRIDE == LANES, "compressed keys are laid out as one 128-lane tile"
    tm_proj = 1024
    tm_trunk = 512
    fchunk = 256
    runs = _proj_runs()
    ovt, eye = _selection_constants(s)
    att_maps, cmp_maps = _bias_maps(s)
    taba = _bias_table(att_maps, rel_bias, True)
    tabc = _bias_table(cmp_maps, rel_bias, False)
    cdt = MXU_DTYPE

    for l in range(DEPTH):
        w_proj = jnp.concatenate(
            [jnp.zeros((D_MODEL, n), cdt) if a < 0 else (w_in[l][:, a:a + n] * sc).astype(cdt)
             for a, n, sc in runs], axis=1)
        pool_bd = _block_diag([pool_w[l, g] for g in range(len(POOL_WINDOWS))]).astype(cdt)
        yac, q, cmp_rows, kv, gates = _proj_in(x, w_proj, conv_w[l], pool_bd, pool_scale[l][None, :], tm_proj)

        w1 = cmp_w1[l].reshape(2, CMP_BLOCK, HEAD_DIM, HEAD_DIM)
        w1bd = jnp.zeros((CMP_BLOCK, LANES, LANES), jnp.float32)
        w1bd = w1bd.at[:, :HEAD_DIM, :HEAD_DIM].set(w1[0]).at[:, HEAD_DIM:, HEAD_DIM:].set(w1[1])
        w1bd = w1bd.reshape(CMP_BLOCK * LANES, LANES).astype(cdt)
        half = CMP_STRIDE * LANES
        pe = jnp.concatenate([cmp_pe[l, 0], cmp_pe[l, 1]], axis=1).reshape(1, CMP_BLOCK * LANES)
        w2bd = _block_diag([cmp_w2[l, 0], cmp_w2[l, 1]]).astype(cdt)
        kcvc = _compress(cmp_rows, pe[:, :half], pe[:, half:], w1bd[:half], w1bd[half:], w2bd)

        yb = _nsa(q, kv, kcvc, gates, taba, tabc, ovt, eye)

        wo = w_out[l]
        woac = jnp.concatenate([wo[0:D_CONV], wo[D_CONV + D_ATTN:]], axis=0).astype(cdt)
        wob = wo[D_CONV:D_CONV + D_ATTN].astype(cdt)
        x = _trunk(
            x.reshape(b * s, D_MODEL), yac.reshape(b * s, -1), yb.reshape(b * s, -1),
            woac, wob, ln1_g[l][None, :], ln1_b[l][None, :],
            w_gate[l].astype(cdt), w_up[l].astype(cdt), w_down[l].astype(cdt),
            ln2_g[l][None, :], ln2_b[l][None, :], tm_trunk, fchunk).reshape(b, s, D_MODEL)
    return x
```

```python
import functools
import math

import numpy as np
import jax
import jax.numpy as jnp
from jax import lax
from jax.experimental import pallas as pl
from jax.experimental.pallas import tpu as pltpu

D_MODEL = 1024
D_CONV = 256
D_POOL = 256
POOL_WINDOWS = (2, 4, 8, 16)
POOL_GROUP = 64
D_ATTN = 512
HEAD_DIM = 64
N_HEADS = 8
N_KV = 2
HPG = 4
CMP_BLOCK = 32
CMP_STRIDE = 16
SEL_BLOCK = 64
TOP_N = 8
WINDOW = 512
N_BRANCH = 3
NUM_BUCKETS = 32
MAX_DISTANCE = 128
D_FF = 2816
DEPTH = 2
ALPHA = (2 * DEPTH) ** 0.25
LN_EPS = 1e-5
NEG = -1e30
FORCE = 1e6

LANES = 128
TQ = 256
VT_ROWS = 80
HALO = 16
VMEM_LIMIT = 56 * 1024 * 1024

MXU_DTYPE = jnp.bfloat16

C_CONV = 0
C_Q = 768
C_CMP = C_Q + N_HEADS * HEAD_DIM
C_KV = C_CMP + N_KV * LANES
C_POOL = C_KV + N_KV * 2 * LANES
C_GATE = C_POOL + D_POOL
N_PROJ = C_GATE + N_KV * LANES

T_D = 0
T_A = 1
T_W = 2
T_FAR = 3
LOG2E = math.log2(math.e)


def _t5_bucket_np(dist):
    n = np.maximum(dist, 0)
    max_exact = NUM_BUCKETS // 2
    nf = np.maximum(n, 1).astype(np.float64)
    val = np.log(nf / max_exact) / math.log(MAX_DISTANCE / max_exact) * (NUM_BUCKETS - max_exact)
    frac = np.abs(val - np.round(val))
    risky = (n > max_exact) & (n < MAX_DISTANCE) & (frac < 1e-6)
    assert not risky.any(), "bucket boundary too close to an integer"
    large = np.minimum(max_exact + np.floor(val + 1e-9).astype(np.int64), NUM_BUCKETS - 1)
    return np.where(n < max_exact, n, large).astype(np.int32)


def _bias_maps(seq):
    assert 2 * TQ == WINDOW
    nc = (seq - CMP_BLOCK) // CMP_STRIDE + 1
    i = np.arange(TQ)[None, :]

    def tile(dist, valid):
        return np.where(valid, _t5_bucket_np(dist), NUM_BUCKETS).astype(np.int32)

    j = np.arange(TQ)[:, None]
    att = []
    for chunks_back in range(3):
        dist = chunks_back * TQ + i - j
        att.append(tile(dist, (dist >= 0) & (dist < WINDOW)))
    att.append(np.full((TQ, TQ), NUM_BUCKETS - 1, np.int32))
    n = np.arange(seq // CMP_STRIDE)[:, None]
    cmp_maps = []
    for qt in range(seq // TQ):
        dist = TQ * qt + i - (CMP_STRIDE * n + CMP_BLOCK - 1)
        cmp_maps.append(tile(dist, (dist >= 0) & (n < nc)))
    return np.stack(att), np.stack(cmp_maps)


def _bias_table_kernel(map_ref, rb_ref, out_ref, *, relative):
    g = pl.program_id(0)
    m = map_ref[0]
    for h in range(HPG):
        head = g * HPG + h
        shift = rb_ref[NUM_BUCKETS - 1, head] if relative else 0.0
        acc = jnp.where(m == NUM_BUCKETS, NEG, 0.0).astype(jnp.float32)
        for b in range(NUM_BUCKETS):
            acc = jnp.where(m == b, (rb_ref[b, head] - shift) * LOG2E, acc)
        out_ref[0, 0, :, h * TQ:(h + 1) * TQ] = acc


def _bias_table(maps, rel_bias, relative):
    nt, rows, _ = maps.shape
    return pl.pallas_call(
        functools.partial(_bias_table_kernel, relative=relative),
        grid=(N_KV, nt),
        in_specs=[
            pl.BlockSpec((1, rows, TQ), lambda g, t: (t, 0, 0)),
            pl.BlockSpec(memory_space=pltpu.SMEM),
        ],
        out_specs=pl.BlockSpec((1, 1, rows, HPG * TQ), lambda g, t: (g, t, 0, 0)),
        out_shape=jax.ShapeDtypeStruct((N_KV, nt, rows, HPG * TQ), jnp.float32),
        name="bias_table",
    )(jnp.asarray(maps), rel_bias)


def _proj_in_kernel(x_ref, w_ref, convw_ref, poolw_ref, pscale_ref,
                    yac_ref, q_ref, cmp_ref, kv_ref, gate_ref, halo_ref, cmps_ref, *, tm):
    st = pl.program_id(1)
    first = st == 0
    xb = x_ref[0].astype(MXU_DTYPE)

    def proj(c0, width):
        return jnp.dot(xb, w_ref[:, c0:c0 + width], preferred_element_type=jnp.float32)

    @pl.when(first)
    def _():
        halo_ref[...] = jnp.zeros(halo_ref.shape, halo_ref.dtype)

    def with_halo(slot, cur):
        prev = halo_ref[slot]
        halo_ref[slot] = cur[tm - HALO:tm, :]
        return jnp.concatenate([prev, cur], axis=0)

    hc = proj(C_CONV, 3 * D_CONV)
    pooled = proj(C_POOL, D_POOL)
    q_ref[0] = proj(C_Q, N_HEADS * HEAD_DIM).astype(q_ref.dtype)
    hcmp = proj(C_CMP, N_KV * LANES)
    kv_ref[0] = proj(C_KV, N_KV * 2 * LANES).astype(kv_ref.dtype)
    gate_ref[0] = jax.nn.sigmoid(proj(C_GATE, N_KV * LANES))

    for g in range(N_KV):
        cmps_ref[g] = hcmp[:, g * LANES:(g + 1) * LANES]
        for r in range(CMP_STRIDE):
            cmp_ref[0, g, :, r * LANES:(r + 1) * LANES] = cmps_ref[g, pl.ds(r, tm // CMP_STRIDE, stride=CMP_STRIDE), :]

    u = hc[:, D_CONV:2 * D_CONV] * hc[:, 2 * D_CONV:3 * D_CONV]
    ue = with_halo(0, u)
    conv = (convw_ref[2:3, :] * ue + convw_ref[1:2, :] * pltpu.roll(ue, 1, 0)
            + convw_ref[0:1, :] * pltpu.roll(ue, 2, 0))
    ya = hc[:, 0:D_CONV] * conv[HALO:, :]
    yac_ref[0, :, 0:D_CONV] = ya.astype(yac_ref.dtype)

    pe = with_halo(1, pooled)
    lane = lax.broadcasted_iota(jnp.int32, pe.shape, 1)
    acc = pe + pltpu.roll(pe, 1, 0)
    for k, shift in enumerate((2, 4, 8)):
        acc = jnp.where(lane >= (k + 1) * POOL_GROUP, acc + pltpu.roll(acc, shift, 0), acc)
    sums = acc[HALO:, :]
    tpos = st * tm + lax.broadcasted_iota(jnp.int32, (tm, D_POOL), 0) + 1
    lane_t = lax.broadcasted_iota(jnp.int32, (tm, D_POOL), 1)
    win = jnp.left_shift(2, jnp.right_shift(lane_t, 6))
    cnt = jnp.minimum(tpos, win).astype(jnp.float32)
    d = sums / cnt - pooled
    yc = jnp.dot(d.astype(MXU_DTYPE), poolw_ref[...], preferred_element_type=jnp.float32) * pscale_ref[...]
    yac_ref[0, :, D_CONV:D_CONV + D_POOL] = yc.astype(yac_ref.dtype)


def _proj_in(x, w_proj, conv_w, pool_bd, pool_scale, tm):
    b, s, _ = x.shape
    kern = functools.partial(_proj_in_kernel, tm=tm)
    const = lambda shape: pl.BlockSpec(shape, lambda i, j: (0,) * len(shape))
    return pl.pallas_call(
        kern,
        grid=(b, s // tm),
        in_specs=[
            pl.BlockSpec((1, tm, D_MODEL), lambda i, j: (i, j, 0)),
            const((D_MODEL, N_PROJ)),
            const((3, D_CONV)),
            const((D_POOL, D_POOL)),
            const((1, D_POOL)),
        ],
        out_specs=[
            pl.BlockSpec((1, tm, D_CONV + D_POOL), lambda i, j: (i, j, 0)),
            pl.BlockSpec((1, tm, N_HEADS * HEAD_DIM), lambda i, j: (i, j, 0)),
            pl.BlockSpec((1, N_KV, tm // CMP_STRIDE, CMP_STRIDE * LANES), lambda i, j: (i, 0, j, 0)),
            pl.BlockSpec((1, tm, N_KV * 2 * LANES), lambda i, j: (i, j, 0)),
            pl.BlockSpec((1, tm, N_KV * LANES), lambda i, j: (i, j, 0)),
        ],
        out_shape=[
            jax.ShapeDtypeStruct((b, s, D_CONV + D_POOL), MXU_DTYPE),
            jax.ShapeDtypeStruct((b, s, N_HEADS * HEAD_DIM), MXU_DTYPE),
            jax.ShapeDtypeStruct((b, N_KV, s // CMP_STRIDE, CMP_STRIDE * LANES), jnp.float32),
            jax.ShapeDtypeStruct((b, s, N_KV * 2 * LANES), MXU_DTYPE),
            jax.ShapeDtypeStruct((b, s, N_KV * LANES), jnp.float32),
        ],
        scratch_shapes=[pltpu.VMEM((2, HALO, D_CONV), jnp.float32),
                        pltpu.VMEM((N_KV, tm, LANES), jnp.float32)],
        compiler_params=pltpu.CompilerParams(
            dimension_semantics=("arbitrary", "arbitrary"), vmem_limit_bytes=VMEM_LIMIT),
        name="proj_in",
    )(x, w_proj, conv_w, pool_bd, pool_scale)


def _compress_kernel(c_ref, pea_ref, peb_ref, w1a_ref, w1b_ref, w2_ref, out_ref):
    c = c_ref[0, 0]
    rows = c.shape[0]
    a = jnp.dot((c + pea_ref[...]).astype(MXU_DTYPE), w1a_ref[...], preferred_element_type=jnp.float32)
    bm = jnp.dot((c + peb_ref[...]).astype(MXU_DTYPE), w1b_ref[...], preferred_element_type=jnp.float32)
    pre = a + pltpu.roll(bm, rows - 1, 0)
    hid = jax.nn.gelu(pre)
    out = jnp.dot(hid.astype(MXU_DTYPE), w2_ref[...], preferred_element_type=jnp.float32)
    out_ref[0, 0] = out.astype(out_ref.dtype)


def _compress(cmp_rows, pea, peb, w1a, w1b, w2bd):
    b, g, rows, width = cmp_rows.shape
    const = lambda shape: pl.BlockSpec(shape, lambda i, j: (0,) * len(shape))
    return pl.pallas_call(
        _compress_kernel,
        grid=(b, g),
        in_specs=[
            pl.BlockSpec((1, 1, rows, width), lambda i, j: (i, j, 0, 0)),
            const((1, width)), const((1, width)),
            const((width, LANES)), const((width, LANES)), const((LANES, LANES)),
        ],
        out_specs=pl.BlockSpec((1, 1, rows, LANES), lambda i, j: (i, j, 0, 0)),
        out_shape=jax.ShapeDtypeStruct((b, g, rows, LANES), MXU_DTYPE),
        compiler_params=pltpu.CompilerParams(
            dimension_semantics=("arbitrary", "arbitrary"), vmem_limit_bytes=VMEM_LIMIT),
        name="compress",
    )(cmp_rows, pea, peb, w1a, w1b, w2bd)


def _dot_nt(a, b, precision=None):
    return lax.dot_general(a, b, (((1,), (1,)), ((), ())), precision=precision,
                           preferred_element_type=jnp.float32)


def _nsa_kernel(q_ref, kv_ref, kc_ref, gate_ref, taba_ref, tabc_ref, ovt_ref, eye_ref,
                out_ref, ka_ref, vt_ref, sc0_ref, sc1_ref, cm0_ref, cm1_ref, m_ref, acc_ref, vs_ref,
                qa_ref, oc_ref,
                *, seq, n_sel):
    nqt = seq // TQ
    nb = seq // SEL_BLOCK
    pad_lane = HEAD_DIM + nb
    cols = HPG * TQ
    dt = ka_ref.dtype
    win_base = TQ + seq
    sc_refs = (sc0_ref, sc1_ref)
    cm_refs = (cm0_ref, cm1_ref)

    lane_k = lax.broadcasted_iota(jnp.int32, (seq, LANES), 1)
    row_k = lax.broadcasted_iota(jnp.int32, (seq, LANES), 0)

    def pad_rows(n):
        lane_p = lax.broadcasted_iota(jnp.int32, (n, LANES), 1)
        return jnp.where(lane_p == pad_lane, 1.0, 0.0).astype(dt)

    kvs = kv_ref[0, :, 0:LANES]
    kvw = kv_ref[0, :, LANES:2 * LANES]
    block_onehot = jnp.where(lane_k == HEAD_DIM + jnp.right_shift(row_k, 6), 1.0, 0.0).astype(dt)
    ka_ref[0:TQ, :] = pad_rows(TQ)
    ka_ref[TQ:win_base, :] = jnp.where(lane_k < HEAD_DIM, kvs, block_onehot)
    ka_ref[win_base:win_base + WINDOW, :] = pad_rows(WINDOW)
    ka_ref[win_base + WINDOW:, :] = jnp.where(lane_k < HEAD_DIM, kvw, jnp.zeros_like(kvw))
    ones_rows = jnp.where(lax.broadcasted_iota(jnp.int32, (VT_ROWS - HEAD_DIM, TQ), 0) == 0, 1.0, 0.0)
    for c in range(nqt):
        for lo, base in ((0, 0), (LANES, nqt)):
            blk = kv_ref[0, c * TQ:(c + 1) * TQ, lo:lo + LANES].astype(jnp.float32)
            vt_ref[base + c] = jnp.concatenate([blk.T[HEAD_DIM:, :], ones_rows], axis=0).astype(dt)
    kcvc = kc_ref[0, 0]
    vct = kcvc.astype(jnp.float32).T[HEAD_DIM:, :].astype(dt)
    ovt = ovt_ref[...].astype(dt)

    def col_max(x):
        return jnp.max(x, axis=0, keepdims=True)

    def prepare(qt):
        s0 = pl.multiple_of(qt * TQ, TQ)
        heads = []
        for p in range(HPG // 2):
            grp = q_ref[0, pl.ds(s0, TQ), p * LANES:(p + 1) * LANES].astype(jnp.float32)
            heads += [grp, pltpu.roll(grp, HEAD_DIM, 1)]
        qs = jnp.concatenate(heads, axis=0)
        lane_r = lax.broadcasted_iota(jnp.int32, (cols, LANES), 1)
        q_cmp = jnp.where(lane_r < HEAD_DIM, qs, 0.0).astype(dt)

        sc = _dot_nt(kcvc, q_cmp) + tabc_ref[0, qt]
        yield
        mc = col_max(sc)
        ec = jnp.where(sc > 0.5 * NEG, jnp.exp2(sc - mc), 0.0)
        lc = jnp.sum(ec, axis=0, keepdims=True)
        pc = (ec / jnp.where(lc > 0.0, lc, 1.0)).astype(dt)
        oc_ref[qt & 1] = jnp.dot(vct, pc, preferred_element_type=jnp.float32)

        imp4 = jnp.dot(ovt, pc, preferred_element_type=jnp.float32)
        yield
        imp = functools.reduce(lambda a, b: a + b, [imp4[:, h * TQ:(h + 1) * TQ] for h in range(HPG)])
        jb = lax.broadcasted_iota(jnp.int32, (nb, TQ), 0)
        qb = qt * (TQ // SEL_BLOCK) + jnp.right_shift(lax.broadcasted_iota(jnp.int32, (nb, TQ), 1), 6)
        valid = jb <= qb
        forced = (jb == 0) | (jb == qb) | (jb == qb - 1)
        v = jnp.where(valid, imp + jnp.where(forced, FORCE, 0.0), NEG)
        vs_ref[...] = v
        sub = lax.broadcasted_iota(jnp.int32, (8, TQ), 0)
        v_grp = [v[8 * k:8 * k + 8] for k in range(nb // 8)]
        r_grp = [jnp.zeros((8, TQ), jnp.int32) for _ in v_grp]
        for jp in range(nb):
            row = vs_ref[jp:jp + 1, :]
            for k, vk in enumerate(v_grp):
                if k > jp // 8:
                    ahead = row >= vk
                elif k < jp // 8:
                    ahead = row > vk
                else:
                    ahead = (row > vk) | ((row == vk) & (sub > jp % 8))
                r_grp[k] = r_grp[k] + ahead.astype(jnp.int32)
        rank = jnp.concatenate(r_grp, axis=0)
        sel_t = jnp.where(valid & (rank < n_sel), 1.0, 0.0)
        sel_t = jnp.concatenate([jnp.zeros((HEAD_DIM, TQ), jnp.float32), sel_t,
                                 jnp.zeros((LANES - HEAD_DIM - nb, TQ), jnp.float32)], axis=0)
        sel_q = _dot_nt(eye_ref[...], sel_t.astype(dt))
        u = (sel_q - 1.0) * (-NEG)
        qa_ref[qt & 1] = jnp.where(lane_r < HEAD_DIM, qs, jnp.concatenate([u] * HPG, axis=0)).astype(dt)

    def item(qt, j):
        n_items = qt + 4
        is_win = (j > qt) & (j < n_items)
        is_pad = j >= n_items
        back = j - (qt + 1)
        chunk = jnp.where(is_win, qt - back, qt - j)
        krow = jnp.where(is_win, win_base + WINDOW + chunk * TQ, (chunk + 1) * TQ)
        krow = jnp.where(is_pad, 0, krow)
        vt_idx = jnp.where(is_win, nqt, 0) + jnp.maximum(chunk, 0)
        vt_idx = jnp.where(is_pad, 0, vt_idx)
        bias_idx = jnp.where(is_win, back, jnp.where(j < T_W, j, T_FAR))
        bias_idx = jnp.where(is_pad, T_FAR, bias_idx)
        branch = jnp.where(is_win | is_pad, 1, 0)
        return pl.multiple_of(krow, TQ), vt_idx, bias_idx, branch

    def scores(qt, j, slot):
        krow, _, bias_idx, _ = item(qt, j)
        s = _dot_nt(ka_ref[pl.ds(krow, TQ), :], qa_ref[qt & 1]) + taba_ref[0, bias_idx]
        sc_refs[slot][...] = s
        cm_refs[slot][...] = jnp.broadcast_to(col_max(s), cm_refs[slot].shape)

    def accumulate(qt, j, slot):
        _, vt_idx, _, branch = item(qt, j)
        m_old = m_ref[branch]
        m_new = jnp.maximum(m_old, cm_refs[slot][...])
        m_ref[branch] = m_new
        p = jnp.exp2(sc_refs[slot][...] - m_new[0:1]).astype(dt)
        acc_ref[branch] = (acc_ref[branch] * jnp.exp2(m_old - m_new)[0:1]
                           + jnp.dot(vt_ref[vt_idx], p, preferred_element_type=jnp.float32))

    def reset_state():
        m_ref[...] = jnp.full(m_ref.shape, NEG, jnp.float32)
        acc_ref[...] = jnp.zeros(acc_ref.shape, jnp.float32)

    def q_tile(qt, carry):
        s0 = pl.multiple_of(qt * TQ, TQ)
        n_pairs = (qt + 5) // 2
        reset_state()
        scores(qt, 0, 0)

        def pair(i):
            scores(qt, 2 * i + 1, 1)
            accumulate(qt, 2 * i, 0)
            scores(qt, 2 * i + 2, 0)
            accumulate(qt, 2 * i + 1, 1)

        def two_pairs(k, c):
            pair(2 * k)
            pair(2 * k + 1)
            return c

        n_loop = n_pairs - 1
        lax.fori_loop(0, n_loop // 2, two_pairs, 0)

        @pl.when(n_loop % 2 == 1)
        def _():
            pair(n_loop - 1)

        last = 2 * (n_pairs - 1)
        prep = prepare(jnp.minimum(qt + 1, nqt - 1))
        next(prep)
        scores(qt, last + 1, 1)
        accumulate(qt, last, 0)
        next(prep)
        accumulate(qt, last + 1, 1)
        for _ in prep:
            pass

        gt = gate_ref[0, pl.ds(s0, TQ), :].T
        o_c = oc_ref[qt & 1]
        mixed = []
        for h in range(HPG):
            cl = slice(h * TQ, (h + 1) * TQ)
            r0 = h * N_BRANCH
            parts = [gt[r0:r0 + 1] * o_c[:, cl]]
            for br in range(2):
                parts.append((gt[r0 + 1 + br:r0 + 2 + br] / acc_ref[br, HEAD_DIM:HEAD_DIM + 1, cl])
                             * acc_ref[br, 0:HEAD_DIM, cl])
            mixed.append(parts[0] + parts[1] + parts[2])
        out_ref[0, pl.ds(s0, TQ), :] = jnp.concatenate(mixed, axis=0).T.astype(out_ref.dtype)
        return carry

    for _ in prepare(0):
        pass
    lax.fori_loop(0, nqt, q_tile, 0)


def _nsa(q, kv, kcvc, gates, taba, tabc, ovt, eye):
    b, s, _ = q.shape
    nb = s // SEL_BLOCK
    nqt = s // TQ
    assert HEAD_DIM + nb < LANES and s % TQ == 0
    kern = functools.partial(_nsa_kernel, seq=s, n_sel=min(TOP_N, nb))
    const = lambda shape: pl.BlockSpec(shape, lambda g, i: (0,) * len(shape))
    return pl.pallas_call(
        kern,
        grid=(N_KV, b),
        in_specs=[
            pl.BlockSpec((1, s, HPG * HEAD_DIM), lambda g, i: (i, 0, g)),
            pl.BlockSpec((1, s, 2 * LANES), lambda g, i: (i, 0, g)),
            pl.BlockSpec((1, 1, s // CMP_STRIDE, LANES), lambda g, i: (i, g, 0, 0)),
            pl.BlockSpec((1, s, LANES), lambda g, i: (i, 0, g)),
            pl.BlockSpec((1,) + taba.shape[1:], lambda g, i: (g, 0, 0, 0)),
            pl.BlockSpec((1,) + tabc.shape[1:], lambda g, i: (g, 0, 0, 0)),
            const(ovt.shape), const(eye.shape),
        ],
        out_specs=pl.BlockSpec((1, s, HPG * HEAD_DIM), lambda g, i: (i, 0, g)),
        out_shape=jax.ShapeDtypeStruct((b, s, D_ATTN), MXU_DTYPE),
        scratch_shapes=[
            pltpu.VMEM((TQ + s + WINDOW + s, LANES), MXU_DTYPE),
            pltpu.VMEM((2 * nqt, VT_ROWS, TQ), MXU_DTYPE),
            pltpu.VMEM((TQ, HPG * TQ), jnp.float32),
            pltpu.VMEM((TQ, HPG * TQ), jnp.float32),
            pltpu.VMEM((8, HPG * TQ), jnp.float32),
            pltpu.VMEM((8, HPG * TQ), jnp.float32),
            pltpu.VMEM((2, 8, HPG * TQ), jnp.float32),
            pltpu.VMEM((2, VT_ROWS, HPG * TQ), jnp.float32),
            pltpu.VMEM((nb, TQ), jnp.float32),
            pltpu.VMEM((2, HPG * TQ, LANES), MXU_DTYPE),
            pltpu.VMEM((2, HEAD_DIM, HPG * TQ), jnp.float32),
        ],
        compiler_params=pltpu.CompilerParams(
            dimension_semantics=("arbitrary", "arbitrary"), vmem_limit_bytes=VMEM_LIMIT),
        name="nsa",
    )(q, kv, kcvc, gates, taba, tabc, ovt, eye)


def _layer_norm(x, g, b):
    mu = jnp.mean(x, axis=-1, keepdims=True)
    xc = x - mu
    var = jnp.mean(xc * xc, axis=-1, keepdims=True)
    return xc * lax.rsqrt(var + LN_EPS) * g + b


def _trunk_kernel(x_ref, yac_ref, yb_ref, woac_ref, wob_ref, g1_ref, b1_ref,
                  wg_ref, wu_ref, wd_ref, g2_ref, b2_ref, out_ref, *, fchunk):
    mix = (jnp.dot(yac_ref[...], woac_ref[...], preferred_element_type=jnp.float32)
           + jnp.dot(yb_ref[...], wob_ref[...], preferred_element_type=jnp.float32))
    x1 = _layer_norm(ALPHA * x_ref[...] + mix, g1_ref[...], b1_ref[...])
    x1b = x1.astype(MXU_DTYPE)
    ffn = jnp.zeros(x1.shape, jnp.float32)
    for c in range(D_FF // fchunk):
        cs = slice(c * fchunk, (c + 1) * fchunk)
        gate = jnp.dot(x1b, wg_ref[:, cs], preferred_element_type=jnp.float32)
        up = jnp.dot(x1b, wu_ref[:, cs], preferred_element_type=jnp.float32)
        act = (jax.nn.silu(gate) * up).astype(MXU_DTYPE)
        ffn = ffn + jnp.dot(act, wd_ref[cs, :], preferred_element_type=jnp.float32)
    out_ref[...] = _layer_norm(ALPHA * x1 + ffn, g2_ref[...], b2_ref[...])


def _trunk(x2, yac2, yb2, woac, wob, g1, b1, wg, wu, wd, g2, b2, tm, fchunk):
    n = x2.shape[0]
    kern = functools.partial(_trunk_kernel, fchunk=fchunk)
    const = lambda shape: pl.BlockSpec(shape, lambda i: (0,) * len(shape), pipeline_mode=pl.Buffered(1))
    rows = lambda width: pl.BlockSpec((tm, width), lambda i: (i, 0))
    return pl.pallas_call(
        kern,
        grid=(n // tm,),
        in_specs=[
            rows(D_MODEL), rows(D_CONV + D_POOL), rows(D_ATTN),
            const(woac.shape), const(wob.shape), const(g1.shape), const(b1.shape),
            const(wg.shape), const(wu.shape), const(wd.shape), const(g2.shape), const(b2.shape),
        ],
        out_specs=rows(D_MODEL),
        out_shape=jax.ShapeDtypeStruct((n, D_MODEL), jnp.float32),
        compiler_params=pltpu.CompilerParams(
            dimension_semantics=("arbitrary",), vmem_limit_bytes=VMEM_LIMIT),
        name="trunk",
    )(x2, yac2, yb2, woac, wob, g1, b1, wg, wu, wd, g2, b2)


def _proj_columns():
    src = np.full((N_PROJ,), -1, np.int64)
    scale = np.ones((N_PROJ,), np.float32)
    src[0:768] = np.arange(768)
    q0, kc0, vc0, ks0, vs0, kw0, vw0, gt0, xp0 = 768, 1280, 1408, 1536, 1664, 1792, 1920, 2048, 2072
    src[C_Q:C_Q + N_HEADS * HEAD_DIM] = q0 + np.arange(N_HEADS * HEAD_DIM)
    scale[C_Q:C_Q + N_HEADS * HEAD_DIM] = HEAD_DIM ** -0.5 * LOG2E
    d = np.arange(HEAD_DIM)
    for g in range(N_KV):
        base = C_CMP + g * LANES
        src[base:base + HEAD_DIM] = kc0 + g * HEAD_DIM + d
        src[base + HEAD_DIM:base + LANES] = vc0 + g * HEAD_DIM + d
        base = C_KV + g * 2 * LANES
        for k, col in enumerate((ks0, vs0, kw0, vw0)):
            src[base + k * HEAD_DIM:base + (k + 1) * HEAD_DIM] = col + g * HEAD_DIM + d
        ng = HPG * N_BRANCH
        src[C_GATE + g * LANES:C_GATE + g * LANES + ng] = gt0 + g * ng + np.arange(ng)
    src[C_POOL:C_POOL + D_POOL] = xp0 + np.arange(D_POOL)
    return src, scale


def _proj_runs():
    src, scale = _proj_columns()
    runs, i = [], 0
    while i < N_PROJ:
        j = i + 1
        while j < N_PROJ and scale[j] == scale[i] and (
                (src[i] < 0 and src[j] < 0) or (src[i] >= 0 and src[j] == src[j - 1] + 1)):
            j += 1
        runs.append((int(src[i]), j - i, float(scale[i])))
        i = j
    return runs


def _block_diag(blocks):
    n = len(blocks)
    r, c = blocks[0].shape
    out = jnp.zeros((n * r, n * c), blocks[0].dtype)
    for i, blk in enumerate(blocks):
        out = out.at[i * r:(i + 1) * r, i * c:(i + 1) * c].set(blk)
    return out


def _selection_constants(seq):
    nb = seq // SEL_BLOCK
    nc = (seq - CMP_BLOCK) // CMP_STRIDE + 1
    cmp_start = np.arange(nc) * CMP_STRIDE
    sel_start = np.arange(nb) * SEL_BLOCK
    ov = np.clip(np.minimum(cmp_start[:, None] + CMP_BLOCK, sel_start[None, :] + SEL_BLOCK)
                 - np.maximum(cmp_start[:, None], sel_start[None, :]), 0, None) / CMP_STRIDE
    ovt = np.zeros((nb, seq // CMP_STRIDE), np.float32)
    ovt[:, :nc] = ov.T
    return jnp.asarray(ovt), jnp.asarray(np.eye(TQ, dtype=np.float32), MXU_DTYPE)


def kernel(x, w_in, conv_w, cmp_pe, cmp_w1, cmp_w2, pool_w, pool_scale, w_out,
           ln1_g, ln1_b, w_gate, w_up, w_down, ln2_g, ln2_b, rel_bias):
    b, s, _ = x.shape
    assert s // CMP_STRIDE == LANES, "compressed keys are laid out as one 128-lane tile"
    tm_proj = 1024
    tm_trunk = 512
    fchunk = 256
    runs = _proj_runs()
    ovt, eye = _selection_constants(s)
    att_maps, cmp_maps = _bias_maps(s)
    taba = _bias_table(att_maps, rel_bias, True)
    tabc = _bias_table(cmp_maps, rel_bias, False)
    cdt = MXU_DTYPE

    for l in range(DEPTH):
        w_proj = jnp.concatenate(
            [jnp.zeros((D_MODEL, n), cdt) if a < 0 else (w_in[l][:, a:a + n] * sc).astype(cdt)
             for a, n, sc in runs], axis=1)
        pool_bd = _block_diag([pool_w[l, g] for g in range(len(POOL_WINDOWS))]).astype(cdt)
        yac, q, cmp_rows, kv, gates = _proj_in(x, w_proj, conv_w[l], pool_bd, pool_scale[l][None, :], tm_proj)

        w1 = cmp_w1[l].reshape(2, CMP_BLOCK, HEAD_DIM, HEAD_DIM)
        w1bd = jnp.zeros((CMP_BLOCK, LANES, LANES), jnp.float32)
        w1bd = w1bd.at[:, :HEAD_DIM, :HEAD_DIM].set(w1[0]).at[:, HEAD_DIM:, HEAD_DIM:].set(w1[1])
        w1bd = w1bd.reshape(CMP_BLOCK * LANES, LANES).astype(cdt)
        half = CMP_STRIDE * LANES
        pe = jnp.concatenate([cmp_pe[l, 0], cmp_pe[l, 1]], axis=1).reshape(1, CMP_BLOCK * LANES)
        w2bd = _block_diag([cmp_w2[l, 0], cmp_w2[l, 1]]).astype(cdt)
        kcvc = _compress(cmp_rows, pe[:, :half], pe[:, half:], w1bd[:half], w1bd[half:], w2bd)

        yb = _nsa(q, kv, kcvc, gates, taba, tabc, ovt, eye)

        wo = w_out[l]
        woac = jnp.concatenate([wo[0:D_CONV], wo[D_CONV + D_ATTN:]], axis=0).astype(cdt)
        wob = wo[D_CONV:D_CONV + D_ATTN].astype(cdt)
        x = _trunk(
            x.reshape(b * s, D_MODEL), yac.reshape(b * s, -1), yb.reshape(b * s, -1),
            woac, wob, ln1_g[l][None, :], ln1_b[l][None, :],
            w_gate[l].astype(cdt), w_up[l].astype(cdt), w_down[l].astype(cdt),
            ln2_g[l][None, :], ln2_b[l][None, :], tm_trunk, fchunk).reshape(b, s, D_MODEL)
    return x
```

```python
import functools
import math

import numpy as np
import jax
import jax.numpy as jnp
from jax import lax
from jax.experimental import pallas as pl
from jax.experimental.pallas import tpu as pltpu

D_MODEL = 1024
D_CONV = 256
D_POOL = 256
POOL_WINDOWS = (2, 4, 8, 16)
POOL_GROUP = 64
D_ATTN = 512
HEAD_DIM = 64
N_HEADS = 8
N_KV = 2
HPG = 4
CMP_BLOCK = 32
CMP_STRIDE = 16
SEL_BLOCK = 64
TOP_N = 8
WINDOW = 512
N_BRANCH = 3
NUM_BUCKETS = 32
MAX_DISTANCE = 128
D_FF = 2816
DEPTH = 2
ALPHA = (2 * DEPTH) ** 0.25
LN_EPS = 1e-5
NEG = -1e30
FORCE = 1e6

LANES = 128
TQ = 256
VT_ROWS = 80
HALO = 16
VMEM_LIMIT = 56 * 1024 * 1024

MXU_DTYPE = jnp.bfloat16

C_CONV = 0
C_Q = 768
C_CMP = C_Q + N_HEADS * HEAD_DIM
C_KV = C_CMP + N_KV * LANES
C_POOL = C_KV + N_KV * 2 * LANES
C_GATE = C_POOL + D_POOL
N_PROJ = C_GATE + N_KV * LANES

T_D = 0
T_A = 1
T_W = 2
T_FAR = 3
LOG2E = math.log2(math.e)


def _t5_bucket_np(dist):
    n = np.maximum(dist, 0)
    max_exact = NUM_BUCKETS // 2
    nf = np.maximum(n, 1).astype(np.float64)
    val = np.log(nf / max_exact) / math.log(MAX_DISTANCE / max_exact) * (NUM_BUCKETS - max_exact)
    frac = np.abs(val - np.round(val))
    risky = (n > max_exact) & (n < MAX_DISTANCE) & (frac < 1e-6)
    assert not risky.any(), "bucket boundary too close to an integer"
    large = np.minimum(max_exact + np.floor(val + 1e-9).astype(np.int64), NUM_BUCKETS - 1)
    return np.where(n < max_exact, n, large).astype(np.int32)


def _bias_maps(seq):
    assert 2 * TQ == WINDOW
    nc = (seq - CMP_BLOCK) // CMP_STRIDE + 1
    i = np.arange(TQ)[None, :]

    def tile(dist, valid):
        return np.where(valid, _t5_bucket_np(dist), NUM_BUCKETS).astype(np.int32)

    j = np.arange(TQ)[:, None]
    att = []
    for chunks_back in range(3):
        dist = chunks_back * TQ + i - j
        att.append(tile(dist, (dist >= 0) & (dist < WINDOW)))
    att.append(np.full((TQ, TQ), NUM_BUCKETS - 1, np.int32))
    n = np.arange(seq // CMP_STRIDE)[:, None]
    cmp_maps = []
    for qt in range(seq // TQ):
        dist = TQ * qt + i - (CMP_STRIDE * n + CMP_BLOCK - 1)
        cmp_maps.append(tile(dist, (dist >= 0) & (n < nc)))
    return np.stack(att), np.stack(cmp_maps)


def _bias_table_kernel(map_ref, rb_ref, out_ref, *, relative):
    g = pl.program_id(0)
    m = map_ref[0]
    for h in range(HPG):
        head = g * HPG + h
        shift = rb_ref[NUM_BUCKETS - 1, head] if relative else 0.0
        acc = jnp.where(m == NUM_BUCKETS, NEG, 0.0).astype(jnp.float32)
        for b in range(NUM_BUCKETS):
            acc = jnp.where(m == b, (rb_ref[b, head] - shift) * LOG2E, acc)
        out_ref[0, 0, :, h * TQ:(h + 1) * TQ] = acc


def _bias_table(maps, rel_bias, relative):
    nt, rows, _ = maps.shape
    return pl.pallas_call(
        functools.partial(_bias_table_kernel, relative=relative),
        grid=(N_KV, nt),
        in_specs=[
            pl.BlockSpec((1, rows, TQ), lambda g, t: (t, 0, 0)),
            pl.BlockSpec(memory_space=pltpu.SMEM),
        ],
        out_specs=pl.BlockSpec((1, 1, rows, HPG * TQ), lambda g, t: (g, t, 0, 0)),
        out_shape=jax.ShapeDtypeStruct((N_KV, nt, rows, HPG * TQ), jnp.float32),
        name="bias_table",
    )(jnp.asarray(maps), rel_bias)


def _proj_in_kernel(x_ref, w_ref, convw_ref, poolw_ref, pscale_ref,
                    yac_ref, q_ref, cmp_ref, kv_ref, gate_ref, halo_ref, cmps_ref, *, tm):
    st = pl.program_id(1)
    first = st == 0
    xb = x_ref[0].astype(MXU_DTYPE)

    def proj(c0, width):
        return jnp.dot(xb, w_ref[:, c0:c0 + width], preferred_element_type=jnp.float32)

    @pl.when(first)
    def _():
        halo_ref[...] = jnp.zeros(halo_ref.shape, halo_ref.dtype)

    def with_halo(slot, cur):
        prev = halo_ref[slot]
        halo_ref[slot] = cur[tm - HALO:tm, :]
        return jnp.concatenate([prev, cur], axis=0)

    hc = proj(C_CONV, 3 * D_CONV)
    pooled = proj(C_POOL, D_POOL)
    q_ref[0] = proj(C_Q, N_HEADS * HEAD_DIM).astype(q_ref.dtype)
    hcmp = proj(C_CMP, N_KV * LANES)
    kv_ref[0] = proj(C_KV, N_KV * 2 * LANES).astype(kv_ref.dtype)
    gate_ref[0] = jax.nn.sigmoid(proj(C_GATE, N_KV * LANES))

    for g in range(N_KV):
        cmps_ref[g] = hcmp[:, g * LANES:(g + 1) * LANES]
        for r in range(CMP_STRIDE):
            cmp_ref[0, g, :, r * LANES:(r + 1) * LANES] = cmps_ref[g, pl.ds(r, tm // CMP_STRIDE, stride=CMP_STRIDE), :]

    u = hc[:, D_CONV:2 * D_CONV] * hc[:, 2 * D_CONV:3 * D_CONV]
    ue = with_halo(0, u)
    conv = (convw_ref[2:3, :] * ue + convw_ref[1:2, :] * pltpu.roll(ue, 1, 0)
            + convw_ref[0:1, :] * pltpu.roll(ue, 2, 0))
    ya = hc[:, 0:D_CONV] * conv[HALO:, :]
    yac_ref[0, :, 0:D_CONV] = ya.astype(yac_ref.dtype)

    pe = with_halo(1, pooled)
    lane = lax.broadcasted_iota(jnp.int32, pe.shape, 1)
    acc = pe + pltpu.roll(pe, 1, 0)
    for k, shift in enumerate((2, 4, 8)):
        acc = jnp.where(lane >= (k + 1) * POOL_GROUP, acc + pltpu.roll(acc, shift, 0), acc)
    sums = acc[HALO:, :]
    tpos = st * tm + lax.broadcasted_iota(jnp.int32, (tm, D_POOL), 0) + 1
    lane_t = lax.broadcasted_iota(jnp.int32, (tm, D_POOL), 1)
    win = jnp.left_shift(2, jnp.right_shift(lane_t, 6))
    cnt = jnp.minimum(tpos, win).astype(jnp.float32)
    d = sums / cnt - pooled
    yc = jnp.dot(d.astype(MXU_DTYPE), poolw_ref[...], preferred_element_type=jnp.float32) * pscale_ref[...]
    yac_ref[0, :, D_CONV:D_CONV + D_POOL] = yc.astype(yac_ref.dtype)


def _proj_in(x, w_proj, conv_w, pool_bd, pool_scale, tm):
    b, s, _ = x.shape
    kern = functools.partial(_proj_in_kernel, tm=tm)
    const = lambda shape: pl.BlockSpec(shape, lambda i, j: (0,) * len(shape))
    return pl.pallas_call(
        kern,
        grid=(b, s // tm),
        in_specs=[
            pl.BlockSpec((1, tm, D_MODEL), lambda i, j: (i, j, 0)),
            const((D_MODEL, N_PROJ)),
            const((3, D_CONV)),
            const((D_POOL, D_POOL)),
            const((1, D_POOL)),
        ],
        out_specs=[
            pl.BlockSpec((1, tm, D_CONV + D_POOL), lambda i, j: (i, j, 0)),
            pl.BlockSpec((1, tm, N_HEADS * HEAD_DIM), lambda i, j: (i, j, 0)),
            pl.BlockSpec((1, N_KV, tm // CMP_STRIDE, CMP_STRIDE * LANES), lambda i, j: (i, 0, j, 0)),
            pl.BlockSpec((1, tm, N_KV * 2 * LANES), lambda i, j: (i, j, 0)),
            pl.BlockSpec((1, tm, N_KV * LANES), lambda i, j: (i, j, 0)),
        ],
        out_shape=[
            jax.ShapeDtypeStruct((b, s, D_CONV + D_POOL), MXU_DTYPE),
            jax.ShapeDtypeStruct((b, s, N_HEADS * HEAD_DIM), MXU_DTYPE),
            jax.ShapeDtypeStruct((b, N_KV, s // CMP_STRIDE, CMP_STRIDE * LANES), jnp.float32),
            jax.ShapeDtypeStruct((b, s, N_KV * 2 * LANES), MXU_DTYPE),
            jax.ShapeDtypeStruct((b, s, N_KV * LANES), jnp.float32),
        ],
        scratch_shapes=[pltpu.VMEM((2, HALO, D_CONV), jnp.float32),
                        pltpu.VMEM((N_KV, tm, LANES), jnp.float32)],
        compiler_params=pltpu.CompilerParams(
            dimension_semantics=("arbitrary", "arbitrary"), vmem_limit_bytes=VMEM_LIMIT),
        name="proj_in",
    )(x, w_proj, conv_w, pool_bd, pool_scale)


def _compress_kernel(c_ref, pea_ref, peb_ref, w1a_ref, w1b_ref, w2_ref, out_ref):
    c = c_ref[...].reshape(-1, c_ref.shape[-1])
    rows = c.shape[0]
    a = jnp.dot((c + pea_ref[...]).astype(MXU_DTYPE), w1a_ref[...], preferred_element_type=jnp.float32)
    bm = jnp.dot((c + peb_ref[...]).astype(MXU_DTYPE), w1b_ref[...], preferred_element_type=jnp.float32)
    pre = a + pltpu.roll(bm, rows - 1, 0)
    hid = jax.nn.gelu(pre)
    out = jnp.dot(hid.astype(MXU_DTYPE), w2_ref[...], preferred_element_type=jnp.float32)
    out_ref[...] = out.astype(out_ref.dtype).reshape(out_ref.shape)


def _compress(cmp_rows, pea, peb, w1a, w1b, w2bd, nbatch):
    b, g, rows, width = cmp_rows.shape
    const = lambda shape: pl.BlockSpec(shape, lambda i: (0,) * len(shape))
    return pl.pallas_call(
        _compress_kernel,
        grid=(b // nbatch,),
        in_specs=[
            pl.BlockSpec((nbatch, g, rows, width), lambda i: (i, 0, 0, 0)),
            const((1, width)), const((1, width)),
            const((width, LANES)), const((width, LANES)), const((LANES, LANES)),
        ],
        out_specs=pl.BlockSpec((nbatch, g, rows, LANES), lambda i: (i, 0, 0, 0)),
        out_shape=jax.ShapeDtypeStruct((b, g, rows, LANES), MXU_DTYPE),
        compiler_params=pltpu.CompilerParams(
            dimension_semantics=("arbitrary",), vmem_limit_bytes=VMEM_LIMIT),
        name="compress",
    )(cmp_rows, pea, peb, w1a, w1b, w2bd)


def _dot_nt(a, b, precision=None):
    return lax.dot_general(a, b, (((1,), (1,)), ((), ())), precision=precision,
                           preferred_element_type=jnp.float32)


def _nsa_kernel(q_ref, kv_ref, kc_ref, gate_ref, taba_ref, tabc_ref, ovt_ref, eye_ref,
                out_ref, ka_ref, vt_ref, sc0_ref, sc1_ref, cm0_ref, cm1_ref, m_ref, acc_ref, vs_ref,
                qa_ref, oc_ref,
                *, seq, n_sel):
    nqt = seq // TQ
    nb = seq // SEL_BLOCK
    pad_lane = HEAD_DIM + nb
    cols = HPG * TQ
    dt = ka_ref.dtype
    win_base = TQ + seq
    sc_refs = (sc0_ref, sc1_ref)
    cm_refs = (cm0_ref, cm1_ref)

    lane_k = lax.broadcasted_iota(jnp.int32, (seq, LANES), 1)
    row_k = lax.broadcasted_iota(jnp.int32, (seq, LANES), 0)

    def pad_rows(n):
        lane_p = lax.broadcasted_iota(jnp.int32, (n, LANES), 1)
        return jnp.where(lane_p == pad_lane, 1.0, 0.0).astype(dt)

    kvs = kv_ref[0, :, 0:LANES]
    kvw = kv_ref[0, :, LANES:2 * LANES]
    block_onehot = jnp.where(lane_k == HEAD_DIM + jnp.right_shift(row_k, 6), 1.0, 0.0).astype(dt)
    ka_ref[0:TQ, :] = pad_rows(TQ)
    ka_ref[TQ:win_base, :] = jnp.where(lane_k < HEAD_DIM, kvs, block_onehot)
    ka_ref[win_base:win_base + WINDOW, :] = pad_rows(WINDOW)
    ka_ref[win_base + WINDOW:, :] = jnp.where(lane_k < HEAD_DIM, kvw, jnp.zeros_like(kvw))
    ones_rows = jnp.where(lax.broadcasted_iota(jnp.int32, (VT_ROWS - HEAD_DIM, TQ), 0) == 0, 1.0, 0.0)
    for c in range(nqt):
        for lo, base in ((0, 0), (LANES, nqt)):
            blk = kv_ref[0, c * TQ:(c + 1) * TQ, lo:lo + LANES].astype(jnp.float32)
            vt_ref[base + c] = jnp.concatenate([blk.T[HEAD_DIM:, :], ones_rows], axis=0).astype(dt)
    kcvc = kc_ref[0, 0]
    vct = kcvc.astype(jnp.float32).T[HEAD_DIM:, :].astype(dt)
    ovt = ovt_ref[...].astype(dt)

    def col_max(x):
        return jnp.max(x, axis=0, keepdims=True)

    def prepare(qt):
        s0 = pl.multiple_of(qt * TQ, TQ)
        heads = []
        for p in range(HPG // 2):
            grp = q_ref[0, pl.ds(s0, TQ), p * LANES:(p + 1) * LANES].astype(jnp.float32)
            heads += [grp, pltpu.roll(grp, HEAD_DIM, 1)]
        qs = jnp.concatenate(heads, axis=0)
        lane_r = lax.broadcasted_iota(jnp.int32, (cols, LANES), 1)
        q_cmp = jnp.where(lane_r < HEAD_DIM, qs, 0.0).astype(dt)

        sc = _dot_nt(kcvc, q_cmp) + tabc_ref[0, qt]
        yield
        mc = col_max(sc)
        ec = jnp.where(sc > 0.5 * NEG, jnp.exp2(sc - mc), 0.0)
        lc = jnp.sum(ec, axis=0, keepdims=True)
        pc = (ec / jnp.where(lc > 0.0, lc, 1.0)).astype(dt)
        oc_ref[qt & 1] = jnp.dot(vct, pc, preferred_element_type=jnp.float32)

        imp4 = jnp.dot(ovt, pc, preferred_element_type=jnp.float32)
        yield
        imp = functools.reduce(lambda a, b: a + b, [imp4[:, h * TQ:(h + 1) * TQ] for h in range(HPG)])
        jb = lax.broadcasted_iota(jnp.int32, (nb, TQ), 0)
        qb = qt * (TQ // SEL_BLOCK) + jnp.right_shift(lax.broadcasted_iota(jnp.int32, (nb, TQ), 1), 6)
        valid = jb <= qb
        forced = (jb == 0) | (jb == qb) | (jb == qb - 1)
        v = jnp.where(valid, imp + jnp.where(forced, FORCE, 0.0), NEG)
        vs_ref[...] = v
        sub = lax.broadcasted_iota(jnp.int32, (8, TQ), 0)
        v_grp = [v[8 * k:8 * k + 8] for k in range(nb // 8)]
        r_grp = [jnp.zeros((8, TQ), jnp.int32) for _ in v_grp]
        for jp in range(nb):
            row = vs_ref[jp:jp + 1, :]
            for k, vk in enumerate(v_grp):
                if k > jp // 8:
                    ahead = row >= vk
                elif k < jp // 8:
                    ahead = row > vk
                else:
                    ahead = (row > vk) | ((row == vk) & (sub > jp % 8))
                r_grp[k] = r_grp[k] + ahead.astype(jnp.int32)
        rank = jnp.concatenate(r_grp, axis=0)
        sel_t = jnp.where(valid & (rank < n_sel), 1.0, 0.0)
        sel_t = jnp.concatenate([jnp.zeros((HEAD_DIM, TQ), jnp.float32), sel_t,
                                 jnp.zeros((LANES - HEAD_DIM - nb, TQ), jnp.float32)], axis=0)
        sel_q = _dot_nt(eye_ref[...], sel_t.astype(dt))
        u = (sel_q - 1.0) * (-NEG)
        qa_ref[qt & 1] = jnp.where(lane_r < HEAD_DIM, qs, jnp.concatenate([u] * HPG, axis=0)).astype(dt)

    def item(qt, j):
        n_items = qt + 4
        is_win = (j > qt) & (j < n_items)
        is_pad = j >= n_items
        back = j - (qt + 1)
        chunk = jnp.where(is_win, qt - back, qt - j)
        krow = jnp.where(is_win, win_base + WINDOW + chunk * TQ, (chunk + 1) * TQ)
        krow = jnp.where(is_pad, 0, krow)
        vt_idx = jnp.where(is_win, nqt, 0) + jnp.maximum(chunk, 0)
        vt_idx = jnp.where(is_pad, 0, vt_idx)
        bias_idx = jnp.where(is_win, back, jnp.where(j < T_W, j, T_FAR))
        bias_idx = jnp.where(is_pad, T_FAR, bias_idx)
        branch = jnp.where(is_win | is_pad, 1, 0)
        return pl.multiple_of(krow, TQ), vt_idx, bias_idx, branch

    def scores(qt, j, slot):
        krow, _, bias_idx, _ = item(qt, j)
        s = _dot_nt(ka_ref[pl.ds(krow, TQ), :], qa_ref[qt & 1]) + taba_ref[0, bias_idx]
        sc_refs[slot][...] = s
        cm_refs[slot][...] = jnp.broadcast_to(col_max(s), cm_refs[slot].shape)

    def accumulate(qt, j, slot):
        _, vt_idx, _, branch = item(qt, j)
        m_old = m_ref[branch]
        m_new = jnp.maximum(m_old, cm_refs[slot][...])
        m_ref[branch] = m_new
        p = jnp.exp2(sc_refs[slot][...] - m_new[0:1]).astype(dt)
        acc_ref[branch] = (acc_ref[branch] * jnp.exp2(m_old - m_new)[0:1]
                           + jnp.dot(vt_ref[vt_idx], p, preferred_element_type=jnp.float32))

    def reset_state():
        m_ref[...] = jnp.full(m_ref.shape, NEG, jnp.float32)
        acc_ref[...] = jnp.zeros(acc_ref.shape, jnp.float32)

    def q_tile(qt, carry):
        s0 = pl.multiple_of(qt * TQ, TQ)
        n_pairs = (qt + 5) // 2
        reset_state()
        scores(qt, 0, 0)

        def pair(i):
            scores(qt, 2 * i + 1, 1)
            accumulate(qt, 2 * i, 0)
            scores(qt, 2 * i + 2, 0)
            accumulate(qt, 2 * i + 1, 1)

        def two_pairs(k, c):
            pair(2 * k)
            pair(2 * k + 1)
            return c

        n_loop = n_pairs - 1
        lax.fori_loop(0, n_loop // 2, two_pairs, 0)

        @pl.when(n_loop % 2 == 1)
        def _():
            pair(n_loop - 1)

        last = 2 * (n_pairs - 1)
        prep = prepare(jnp.minimum(qt + 1, nqt - 1))
        next(prep)
        scores(qt, last + 1, 1)
        accumulate(qt, last, 0)
        next(prep)
        accumulate(qt, last + 1, 1)
        for _ in prep:
            pass

        gt = gate_ref[0, pl.ds(s0, TQ), :].T
        o_c = oc_ref[qt & 1]
        mixed = []
        for h in range(HPG):
            cl = slice(h * TQ, (h + 1) * TQ)
            r0 = h * N_BRANCH
            parts = [gt[r0:r0 + 1] * o_c[:, cl]]
            for br in range(2):
                parts.append((gt[r0 + 1 + br:r0 + 2 + br] / acc_ref[br, HEAD_DIM:HEAD_DIM + 1, cl])
                             * acc_ref[br, 0:HEAD_DIM, cl])
            mixed.append(parts[0] + parts[1] + parts[2])
        out_ref[0, pl.ds(s0, TQ), :] = jnp.concatenate(mixed, axis=0).T.astype(out_ref.dtype)
        return carry

    for _ in prepare(0):
        pass
    lax.fori_loop(0, nqt, q_tile, 0)


def _nsa(q, kv, kcvc, gates, taba, tabc, ovt, eye):
    b, s, _ = q.shape
    nb = s // SEL_BLOCK
    nqt = s // TQ
    assert HEAD_DIM + nb < LANES and s % TQ == 0
    kern = functools.partial(_nsa_kernel, seq=s, n_sel=min(TOP_N, nb))
    const = lambda shape: pl.BlockSpec(shape, lambda g, i: (0,) * len(shape))
    return pl.pallas_call(
        kern,
        grid=(N_KV, b),
        in_specs=[
            pl.BlockSpec((1, s, HPG * HEAD_DIM), lambda g, i: (i, 0, g)),
            pl.BlockSpec((1, s, 2 * LANES), lambda g, i: (i, 0, g)),
            pl.BlockSpec((1, 1, s // CMP_STRIDE, LANES), lambda g, i: (i, g, 0, 0)),
            pl.BlockSpec((1, s, LANES), lambda g, i: (i, 0, g)),
            pl.BlockSpec((1,) + taba.shape[1:], lambda g, i: (g, 0, 0, 0)),
            pl.BlockSpec((1,) + tabc.shape[1:], lambda g, i: (g, 0, 0, 0)),
            const(ovt.shape), const(eye.shape),
        ],
        out_specs=pl.BlockSpec((1, s, HPG * HEAD_DIM), lambda g, i: (i, 0, g)),
        out_shape=jax.ShapeDtypeStruct((b, s, D_ATTN), MXU_DTYPE),
        scratch_shapes=[
            pltpu.VMEM((TQ + s + WINDOW + s, LANES), MXU_DTYPE),
            pltpu.VMEM((2 * nqt, VT_ROWS, TQ), MXU_DTYPE),
            pltpu.VMEM((TQ, HPG * TQ), jnp.float32),
            pltpu.VMEM((TQ, HPG * TQ), jnp.float32),
            pltpu.VMEM((8, HPG * TQ), jnp.float32),
            pltpu.VMEM((8, HPG * TQ), jnp.float32),
            pltpu.VMEM((2, 8, HPG * TQ), jnp.float32),
            pltpu.VMEM((2, VT_ROWS, HPG * TQ), jnp.float32),
            pltpu.VMEM((nb, TQ), jnp.float32),
            pltpu.VMEM((2, HPG * TQ, LANES), MXU_DTYPE),
            pltpu.VMEM((2, HEAD_DIM, HPG * TQ), jnp.float32),
        ],
        compiler_params=pltpu.CompilerParams(
            dimension_semantics=("arbitrary", "arbitrary"), vmem_limit_bytes=VMEM_LIMIT),
        name="nsa",
    )(q, kv, kcvc, gates, taba, tabc, ovt, eye)


def _layer_norm(x, g, b):
    mu = jnp.mean(x, axis=-1, keepdims=True)
    xc = x - mu
    var = jnp.mean(xc * xc, axis=-1, keepdims=True)
    return xc * lax.rsqrt(var + LN_EPS) * g + b


def _trunk_kernel(x_ref, yac_ref, yb_ref, woac_ref, wob_ref, g1_ref, b1_ref,
                  wg_ref, wu_ref, wd_ref, g2_ref, b2_ref, out_ref, *, fchunk):
    mix = (jnp.dot(yac_ref[...], woac_ref[...], preferred_element_type=jnp.float32)
           + jnp.dot(yb_ref[...], wob_ref[...], preferred_element_type=jnp.float32))
    x1 = _layer_norm(ALPHA * x_ref[...] + mix, g1_ref[...], b1_ref[...])
    x1b = x1.astype(MXU_DTYPE)
    ffn = jnp.zeros(x1.shape, jnp.float32)
    for c in range(D_FF // fchunk):
        cs = slice(c * fchunk, (c + 1) * fchunk)
        gate = jnp.dot(x1b, wg_ref[:, cs], preferred_element_type=jnp.float32)
        up = jnp.dot(x1b, wu_ref[:, cs], preferred_element_type=jnp.float32)
        act = (jax.nn.silu(gate) * up).astype(MXU_DTYPE)
        ffn = ffn + jnp.dot(act, wd_ref[cs, :], preferred_element_type=jnp.float32)
    out_ref[...] = _layer_norm(ALPHA * x1 + ffn, g2_ref[...], b2_ref[...])


def _trunk(x2, yac2, yb2, woac, wob, g1, b1, wg, wu, wd, g2, b2, tm, fchunk):
    n = x2.shape[0]
    kern = functools.partial(_trunk_kernel, fchunk=fchunk)
    const = lambda shape: pl.BlockSpec(shape, lambda i: (0,) * len(shape), pipeline_mode=pl.Buffered(1))
    rows = lambda width: pl.BlockSpec((tm, width), lambda i: (i, 0))
    return pl.pallas_call(
        kern,
        grid=(n // tm,),
        in_specs=[
            rows(D_MODEL), rows(D_CONV + D_POOL), rows(D_ATTN),
            const(woac.shape), const(wob.shape), const(g1.shape), const(b1.shape),
            const(wg.shape), const(wu.shape), const(wd.shape), const(g2.shape), const(b2.shape),
        ],
        out_specs=rows(D_MODEL),
        out_shape=jax.ShapeDtypeStruct((n, D_MODEL), jnp.float32),
        compiler_params=pltpu.CompilerParams(
            dimension_semantics=("arbitrary",), vmem_limit_bytes=VMEM_LIMIT),
        name="trunk",
    )(x2, yac2, yb2, woac, wob, g1, b1, wg, wu, wd, g2, b2)


def _proj_columns():
    src = np.full((N_PROJ,), -1, np.int64)
    scale = np.ones((N_PROJ,), np.float32)
    src[0:768] = np.arange(768)
    q0, kc0, vc0, ks0, vs0, kw0, vw0, gt0, xp0 = 768, 1280, 1408, 1536, 1664, 1792, 1920, 2048, 2072
    src[C_Q:C_Q + N_HEADS * HEAD_DIM] = q0 + np.arange(N_HEADS * HEAD_DIM)
    scale[C_Q:C_Q + N_HEADS * HEAD_DIM] = HEAD_DIM ** -0.5 * LOG2E
    d = np.arange(HEAD_DIM)
    for g in range(N_KV):
        base = C_CMP + g * LANES
        src[base:base + HEAD_DIM] = kc0 + g * HEAD_DIM + d
        src[base + HEAD_DIM:base + LANES] = vc0 + g * HEAD_DIM + d
        base = C_KV + g * 2 * LANES
        for k, col in enumerate((ks0, vs0, kw0, vw0)):
            src[base + k * HEAD_DIM:base + (k + 1) * HEAD_DIM] = col + g * HEAD_DIM + d
        ng = HPG * N_BRANCH
        src[C_GATE + g * LANES:C_GATE + g * LANES + ng] = gt0 + g * ng + np.arange(ng)
    src[C_POOL:C_POOL + D_POOL] = xp0 + np.arange(D_POOL)
    return src, scale


def _proj_runs():
    src, scale = _proj_columns()
    runs, i = [], 0
    while i < N_PROJ:
        j = i + 1
        while j < N_PROJ and scale[j] == scale[i] and (
                (src[i] < 0 and src[j] < 0) or (src[i] >= 0 and src[j] == src[j - 1] + 1)):
            j += 1
        runs.append((int(src[i]), j - i, float(scale[i])))
        i = j
    return runs


def _block_diag(blocks):
    n = len(blocks)
    r, c = blocks[0].shape
    out = jnp.zeros((n * r, n * c), blocks[0].dtype)
    for i, blk in enumerate(blocks):
        out = out.at[i * r:(i + 1) * r, i * c:(i + 1) * c].set(blk)
    return out


def _selection_constants(seq):
    nb = seq // SEL_BLOCK
    nc = (seq - CMP_BLOCK) // CMP_STRIDE + 1
    cmp_start = np.arange(nc) * CMP_STRIDE
    sel_start = np.arange(nb) * SEL_BLOCK
    ov = np.clip(np.minimum(cmp_start[:, None] + CMP_BLOCK, sel_start[None, :] + SEL_BLOCK)
                 - np.maximum(cmp_start[:, None], sel_start[None, :]), 0, None) / CMP_STRIDE
    ovt = np.zeros((nb, seq // CMP_STRIDE), np.float32)
    ovt[:, :nc] = ov.T
    return jnp.asarray(ovt), jnp.asarray(np.eye(TQ, dtype=np.float32), MXU_DTYPE)


def kernel(x, w_in, conv_w, cmp_pe, cmp_w1, cmp_w2, pool_w, pool_scale, w_out,
           ln1_g, ln1_b, w_gate, w_up, w_down, ln2_g, ln2_b, rel_bias):
    b, s, _ = x.shape
    assert s // CMP_STRIDE == LANES, "compressed keys are laid out as one 128-lane tile"
    tm_proj = 1024
    tm_trunk = 512
    nb_compress = 2 if b % 2 == 0 else 1
    fchunk = 256
    runs = _proj_runs()
    ovt, eye = _selection_constants(s)
    att_maps, cmp_maps = _bias_maps(s)
    taba = _bias_table(att_maps, rel_bias, True)
    tabc = _bias_table(cmp_maps, rel_bias, False)
    cdt = MXU_DTYPE

    for l in range(DEPTH):
        w_proj = jnp.concatenate(
            [jnp.zeros((D_MODEL, n), cdt) if a < 0 else (w_in[l][:, a:a + n] * sc).astype(cdt)
             for a, n, sc in runs], axis=1)
        pool_bd = _block_diag([pool_w[l, g] for g in range(len(POOL_WINDOWS))]).astype(cdt)
        yac, q, cmp_rows, kv, gates = _proj_in(x, w_proj, conv_w[l], pool_bd, pool_scale[l][None, :], tm_proj)

        w1 = cmp_w1[l].reshape(2, CMP_BLOCK, HEAD_DIM, HEAD_DIM)
        zero = jnp.zeros_like(w1[0])
        w1bd = jnp.concatenate([jnp.concatenate([w1[0], zero], axis=2),
                                jnp.concatenate([zero, w1[1]], axis=2)], axis=1)
        w1bd = w1bd.reshape(CMP_BLOCK * LANES, LANES).astype(cdt)
        half = CMP_STRIDE * LANES
        pe = jnp.concatenate([cmp_pe[l, 0], cmp_pe[l, 1]], axis=1).reshape(1, CMP_BLOCK * LANES)
        w2bd = _block_diag([cmp_w2[l, 0], cmp_w2[l, 1]]).astype(cdt)
        kcvc = _compress(cmp_rows, pe[:, :half], pe[:, half:], w1bd[:half], w1bd[half:], w2bd, nb_compress)

        yb = _nsa(q, kv, kcvc, gates, taba, tabc, ovt, eye)

        wo = w_out[l]
        woac = jnp.concatenate([wo[0:D_CONV], wo[D_CONV + D_ATTN:]], axis=0).astype(cdt)
        wob = wo[D_CONV:D_CONV + D_ATTN].astype(cdt)
        x = _trunk(
            x.reshape(b * s, D_MODEL), yac.reshape(b * s, -1), yb.reshape(b * s, -1),
            woac, wob, ln1_g[l][None, :], ln1_b[l][None, :],
            w_gate[l].astype(cdt), w_up[l].astype(cdt), w_down[l].astype(cdt),
            ln2_g[l][None, :], ln2_b[l][None, :], tm_trunk, fchunk).reshape(b, s, D_MODEL)
    return x
```

```python
import functools
import math

import numpy as np
import jax
import jax.numpy as jnp
from jax import lax
from jax.experimental import pallas as pl
from jax.experimental.pallas import tpu as pltpu

D_MODEL = 1024
D_CONV = 256
D_POOL = 256
POOL_WINDOWS = (2, 4, 8, 16)
POOL_GROUP = 64
D_ATTN = 512
HEAD_DIM = 64
N_HEADS = 8
N_KV = 2
HPG = 4
CMP_BLOCK = 32
CMP_STRIDE = 16
SEL_BLOCK = 64
TOP_N = 8
WINDOW = 512
N_BRANCH = 3
NUM_BUCKETS = 32
MAX_DISTANCE = 128
D_FF = 2816
DEPTH = 2
ALPHA = (2 * DEPTH) ** 0.25
LN_EPS = 1e-5
NEG = -1e30
FORCE = 1e6

LANES = 128
TQ = 256
VT_ROWS = 80
HALO = 16
VMEM_LIMIT = 56 * 1024 * 1024

MXU_DTYPE = jnp.bfloat16

C_CONV = 0
C_Q = 768
C_CMP = C_Q + N_HEADS * HEAD_DIM
C_KV = C_CMP + N_KV * LANES
C_POOL = C_KV + N_KV * LANES
C_GATE = C_POOL + D_POOL
N_PROJ = C_GATE + N_KV * LANES
V_COLS = (1664, 1920)

T_D = 0
T_A = 1
T_W = 2
T_FAR = 3
LOG2E = math.log2(math.e)


def _t5_bucket_np(dist):
    n = np.maximum(dist, 0)
    max_exact = NUM_BUCKETS // 2
    nf = np.maximum(n, 1).astype(np.float64)
    val = np.log(nf / max_exact) / math.log(MAX_DISTANCE / max_exact) * (NUM_BUCKETS - max_exact)
    frac = np.abs(val - np.round(val))
    risky = (n > max_exact) & (n < MAX_DISTANCE) & (frac < 1e-6)
    assert not risky.any(), "bucket boundary too close to an integer"
    large = np.minimum(max_exact + np.floor(val + 1e-9).astype(np.int64), NUM_BUCKETS - 1)
    return np.where(n < max_exact, n, large).astype(np.int32)


def _bias_maps(seq):
    assert 2 * TQ == WINDOW
    nc = (seq - CMP_BLOCK) // CMP_STRIDE + 1
    i = np.arange(TQ)[None, :]

    def tile(dist, valid):
        return np.where(valid, _t5_bucket_np(dist), NUM_BUCKETS).astype(np.int32)

    j = np.arange(TQ)[:, None]
    att = []
    for chunks_back in range(3):
        dist = chunks_back * TQ + i - j
        att.append(tile(dist, (dist >= 0) & (dist < WINDOW)))
    att.append(np.full((TQ, TQ), NUM_BUCKETS - 1, np.int32))
    n = np.arange(seq // CMP_STRIDE)[:, None]
    cmp_maps = []
    for qt in range(seq // TQ):
        dist = TQ * qt + i - (CMP_STRIDE * n + CMP_BLOCK - 1)
        cmp_maps.append(tile(dist, (dist >= 0) & (n < nc)))
    return np.stack(att), np.stack(cmp_maps)


def _bias_table_kernel(map_ref, rb_ref, out_ref, *, relative):
    g = pl.program_id(0)
    m = map_ref[0]
    for h in range(HPG):
        head = g * HPG + h
        shift = rb_ref[NUM_BUCKETS - 1, head] if relative else 0.0
        acc = jnp.where(m == NUM_BUCKETS, NEG, 0.0).astype(jnp.float32)
        for b in range(NUM_BUCKETS):
            acc = jnp.where(m == b, (rb_ref[b, head] - shift) * LOG2E, acc)
        out_ref[0, 0, :, h * TQ:(h + 1) * TQ] = acc


def _bias_table(maps, rel_bias, relative):
    nt, rows, _ = maps.shape
    return pl.pallas_call(
        functools.partial(_bias_table_kernel, relative=relative),
        grid=(N_KV, nt),
        in_specs=[
            pl.BlockSpec((1, rows, TQ), lambda g, t: (t, 0, 0)),
            pl.BlockSpec(memory_space=pltpu.SMEM),
        ],
        out_specs=pl.BlockSpec((1, 1, rows, HPG * TQ), lambda g, t: (g, t, 0, 0)),
        out_shape=jax.ShapeDtypeStruct((N_KV, nt, rows, HPG * TQ), jnp.float32),
        name="bias_table",
    )(jnp.asarray(maps), rel_bias)


def _dot_nt(a, b, precision=None):
    return lax.dot_general(a, b, (((1,), (1,)), ((), ())), precision=precision,
                           preferred_element_type=jnp.float32)


def _proj_in_kernel(x_ref, w_ref, wvt_ref, convw_ref, poolw_ref, pscale_ref,
                    yac_ref, q_ref, cmp_ref, ka_ref, vt_ref, gate_ref, halo_ref, cmps_ref, *, tm, n_blocks):
    st = pl.program_id(1)
    first = st == 0
    xb = x_ref[0].astype(MXU_DTYPE)

    def proj(c0, width):
        return jnp.dot(xb, w_ref[:, c0:c0 + width], preferred_element_type=jnp.float32)

    @pl.when(first)
    def _():
        halo_ref[...] = jnp.zeros(halo_ref.shape, halo_ref.dtype)

    def with_halo(slot, cur):
        prev = halo_ref[slot]
        halo_ref[slot] = cur[tm - HALO:tm, :]
        return jnp.concatenate([prev, cur], axis=0)

    hc = proj(C_CONV, 3 * D_CONV)
    pooled = proj(C_POOL, D_POOL)
    q_ref[0] = proj(C_Q, N_HEADS * HEAD_DIM).astype(q_ref.dtype)
    hcmp = proj(C_CMP, N_KV * LANES)
    hk = proj(C_KV, N_KV * LANES)
    gate_ref[0] = jax.nn.sigmoid(proj(C_GATE, N_KV * LANES))
    vts = _dot_nt(wvt_ref[...], xb)

    lane_k = lax.broadcasted_iota(jnp.int32, (tm, LANES), 1)
    blk_k = jnp.right_shift(st * tm + lax.broadcasted_iota(jnp.int32, (tm, LANES), 0), 6)
    onehot = jnp.where(lane_k == HEAD_DIM + blk_k, 1.0, 0.0)
    assert HEAD_DIM + n_blocks < LANES
    ones_rows = jnp.where(lax.broadcasted_iota(jnp.int32, (VT_ROWS - HEAD_DIM, TQ), 0) == 0, 1.0, 0.0)
    for g in range(N_KV):
        slab = hk[:, g * LANES:(g + 1) * LANES]
        ka_ref[0, g, 0] = jnp.where(lane_k < HEAD_DIM, slab, onehot).astype(ka_ref.dtype)
        ka_ref[0, g, 1] = jnp.where(lane_k < HEAD_DIM, pltpu.roll(slab, HEAD_DIM, 1), 0.0).astype(ka_ref.dtype)
        for br in range(2):
            r0 = (2 * g + br) * HEAD_DIM
            for c in range(tm // TQ):
                vt_ref[0, g, br, c] = jnp.concatenate(
                    [vts[r0:r0 + HEAD_DIM, c * TQ:(c + 1) * TQ], ones_rows], axis=0).astype(vt_ref.dtype)

    for g in range(N_KV):
        cmps_ref[g] = hcmp[:, g * LANES:(g + 1) * LANES]
        for r in range(CMP_STRIDE):
            cmp_ref[0, g, :, r * LANES:(r + 1) * LANES] = cmps_ref[g, pl.ds(r, tm // CMP_STRIDE, stride=CMP_STRIDE), :]

    u = hc[:, D_CONV:2 * D_CONV] * hc[:, 2 * D_CONV:3 * D_CONV]
    ue = with_halo(0, u)
    conv = (convw_ref[2:3, :] * ue + convw_ref[1:2, :] * pltpu.roll(ue, 1, 0)
            + convw_ref[0:1, :] * pltpu.roll(ue, 2, 0))
    ya = hc[:, 0:D_CONV] * conv[HALO:, :]
    yac_ref[0, :, 0:D_CONV] = ya.astype(yac_ref.dtype)

    pe = with_halo(1, pooled)
    lane = lax.broadcasted_iota(jnp.int32, pe.shape, 1)
    acc = pe + pltpu.roll(pe, 1, 0)
    for k, shift in enumerate((2, 4, 8)):
        acc = jnp.where(lane >= (k + 1) * POOL_GROUP, acc + pltpu.roll(acc, shift, 0), acc)
    sums = acc[HALO:, :]
    tpos = st * tm + lax.broadcasted_iota(jnp.int32, (tm, D_POOL), 0) + 1
    lane_t = lax.broadcasted_iota(jnp.int32, (tm, D_POOL), 1)
    win = jnp.left_shift(2, jnp.right_shift(lane_t, 6))
    cnt = jnp.minimum(tpos, win).astype(jnp.float32)
    d = sums / cnt - pooled
    yc = jnp.dot(d.astype(MXU_DTYPE), poolw_ref[...], preferred_element_type=jnp.float32) * pscale_ref[...]
    yac_ref[0, :, D_CONV:D_CONV + D_POOL] = yc.astype(yac_ref.dtype)


def _proj_in(x, w_proj, wvt, conv_w, pool_bd, pool_scale, tm):
    b, s, _ = x.shape
    kern = functools.partial(_proj_in_kernel, tm=tm, n_blocks=s // SEL_BLOCK)
    const = lambda shape: pl.BlockSpec(shape, lambda i, j: (0,) * len(shape))
    return pl.pallas_call(
        kern,
        grid=(b, s // tm),
        in_specs=[
            pl.BlockSpec((1, tm, D_MODEL), lambda i, j: (i, j, 0)),
            const((D_MODEL, N_PROJ)),
            const(wvt.shape),
            const((3, D_CONV)),
            const((D_POOL, D_POOL)),
            const((1, D_POOL)),
        ],
        out_specs=[
            pl.BlockSpec((1, tm, D_CONV + D_POOL), lambda i, j: (i, j, 0)),
            pl.BlockSpec((1, tm, N_HEADS * HEAD_DIM), lambda i, j: (i, j, 0)),
            pl.BlockSpec((1, N_KV, tm // CMP_STRIDE, CMP_STRIDE * LANES), lambda i, j: (i, 0, j, 0)),
            pl.BlockSpec((1, N_KV, 2, tm, LANES), lambda i, j: (i, 0, 0, j, 0)),
            pl.BlockSpec((1, N_KV, 2, tm // TQ, VT_ROWS, TQ), lambda i, j: (i, 0, 0, j, 0, 0)),
            pl.BlockSpec((1, tm, N_KV * LANES), lambda i, j: (i, j, 0)),
        ],
        out_shape=[
            jax.ShapeDtypeStruct((b, s, D_CONV + D_POOL), MXU_DTYPE),
            jax.ShapeDtypeStruct((b, s, N_HEADS * HEAD_DIM), MXU_DTYPE),
            jax.ShapeDtypeStruct((b, N_KV, s // CMP_STRIDE, CMP_STRIDE * LANES), jnp.float32),
            jax.ShapeDtypeStruct((b, N_KV, 2, s, LANES), MXU_DTYPE),
            jax.ShapeDtypeStruct((b, N_KV, 2, s // TQ, VT_ROWS, TQ), MXU_DTYPE),
            jax.ShapeDtypeStruct((b, s, N_KV * LANES), jnp.float32),
        ],
        scratch_shapes=[pltpu.VMEM((2, HALO, D_CONV), jnp.float32),
                        pltpu.VMEM((N_KV, tm, LANES), jnp.float32)],
        compiler_params=pltpu.CompilerParams(
            dimension_semantics=("arbitrary", "arbitrary"), vmem_limit_bytes=VMEM_LIMIT),
        name="proj_in",
    )(x, w_proj, wvt, conv_w, pool_bd, pool_scale)


def _compress_kernel(c_ref, pea_ref, peb_ref, w1a_ref, w1b_ref, w2_ref, out_ref):
    c = c_ref[...].reshape(-1, c_ref.shape[-1])
    rows = c.shape[0]
    a = jnp.dot((c + pea_ref[...]).astype(MXU_DTYPE), w1a_ref[...], preferred_element_type=jnp.float32)
    bm = jnp.dot((c + peb_ref[...]).astype(MXU_DTYPE), w1b_ref[...], preferred_element_type=jnp.float32)
    pre = a + pltpu.roll(bm, rows - 1, 0)
    hid = jax.nn.gelu(pre)
    out = jnp.dot(hid.astype(MXU_DTYPE), w2_ref[...], preferred_element_type=jnp.float32)
    out_ref[...] = out.astype(out_ref.dtype).reshape(out_ref.shape)


def _compress(cmp_rows, pea, peb, w1a, w1b, w2bd, nbatch):
    b, g, rows, width = cmp_rows.shape
    const = lambda shape: pl.BlockSpec(shape, lambda i: (0,) * len(shape))
    return pl.pallas_call(
        _compress_kernel,
        grid=(b // nbatch,),
        in_specs=[
            pl.BlockSpec((nbatch, g, rows, width), lambda i: (i, 0, 0, 0)),
            const((1, width)), const((1, width)),
            const((width, LANES)), const((width, LANES)), const((LANES, LANES)),
        ],
        out_specs=pl.BlockSpec((nbatch, g, rows, LANES), lambda i: (i, 0, 0, 0)),
        out_shape=jax.ShapeDtypeStruct((b, g, rows, LANES), MXU_DTYPE),
        compiler_params=pltpu.CompilerParams(
            dimension_semantics=("arbitrary",), vmem_limit_bytes=VMEM_LIMIT),
        name="compress",
    )(cmp_rows, pea, peb, w1a, w1b, w2bd)


def _nsa_kernel(q_ref, kin_ref, vt_ref, kc_ref, gate_ref, taba_ref, tabc_ref, ovt_ref, eye_ref,
                out_ref, ka_ref, sc0_ref, sc1_ref, cm0_ref, cm1_ref, m_ref, acc_ref, vs_ref,
                qa_ref, oc_ref,
                *, seq, n_sel):
    nqt = seq // TQ
    nb = seq // SEL_BLOCK
    pad_lane = HEAD_DIM + nb
    cols = HPG * TQ
    dt = ka_ref.dtype
    win_base = TQ + seq
    sc_refs = (sc0_ref, sc1_ref)
    cm_refs = (cm0_ref, cm1_ref)

    def pad_rows(n):
        lane_p = lax.broadcasted_iota(jnp.int32, (n, LANES), 1)
        return jnp.where(lane_p == pad_lane, 1.0, 0.0).astype(dt)

    ka_ref[0:TQ, :] = pad_rows(TQ)
    ka_ref[TQ:win_base, :] = kin_ref[0, 0, 0]
    ka_ref[win_base:win_base + WINDOW, :] = pad_rows(WINDOW)
    ka_ref[win_base + WINDOW:, :] = kin_ref[0, 0, 1]
    kcvc = kc_ref[0, 0]
    vct = kcvc.astype(jnp.float32).T[HEAD_DIM:, :].astype(dt)
    ovt = ovt_ref[...].astype(dt)

    def col_max(x):
        return jnp.max(x, axis=0, keepdims=True)

    def prepare(qt):
        s0 = pl.multiple_of(qt * TQ, TQ)
        heads = []
        for p in range(HPG // 2):
            grp = q_ref[0, pl.ds(s0, TQ), p * LANES:(p + 1) * LANES].astype(jnp.float32)
            heads += [grp, pltpu.roll(grp, HEAD_DIM, 1)]
        qs = jnp.concatenate(heads, axis=0)
        lane_r = lax.broadcasted_iota(jnp.int32, (cols, LANES), 1)
        q_cmp = jnp.where(lane_r < HEAD_DIM, qs, 0.0).astype(dt)

        sc = _dot_nt(kcvc, q_cmp) + tabc_ref[0, qt]
        yield
        mc = col_max(sc)
        ec = jnp.where(sc > 0.5 * NEG, jnp.exp2(sc - mc), 0.0)
        lc = jnp.sum(ec, axis=0, keepdims=True)
        pc = (ec / jnp.where(lc > 0.0, lc, 1.0)).astype(dt)
        oc_ref[qt & 1] = jnp.dot(vct, pc, preferred_element_type=jnp.float32)

        imp4 = jnp.dot(ovt, pc, preferred_element_type=jnp.float32)
        yield
        imp = functools.reduce(lambda a, b: a + b, [imp4[:, h * TQ:(h + 1) * TQ] for h in range(HPG)])
        jb = lax.broadcasted_iota(jnp.int32, (nb, TQ), 0)
        qb = qt * (TQ // SEL_BLOCK) + jnp.right_shift(lax.broadcasted_iota(jnp.int32, (nb, TQ), 1), 6)
        valid = jb <= qb
        forced = (jb == 0) | (jb == qb) | (jb == qb - 1)
        v = jnp.where(valid, imp + jnp.where(forced, FORCE, 0.0), NEG)
        vs_ref[...] = v
        sub = lax.broadcasted_iota(jnp.int32, (8, TQ), 0)
        v_grp = [v[8 * k:8 * k + 8] for k in range(nb // 8)]
        r_grp = [jnp.zeros((8, TQ), jnp.int32) for _ in v_grp]
        for jp in range(nb):
            row = vs_ref[jp:jp + 1, :]
            for k, vk in enumerate(v_grp):
                if k > jp // 8:
                    ahead = row >= vk
                elif k < jp // 8:
                    ahead = row > vk
                else:
                    ahead = (row > vk) | ((row == vk) & (sub > jp % 8))
                r_grp[k] = r_grp[k] + ahead.astype(jnp.int32)
        rank = jnp.concatenate(r_grp, axis=0)
        sel_t = jnp.where(valid & (rank < n_sel), 1.0, 0.0)
        sel_t = jnp.concatenate([jnp.zeros((HEAD_DIM, TQ), jnp.float32), sel_t,
                                 jnp.zeros((LANES - HEAD_DIM - nb, TQ), jnp.float32)], axis=0)
        sel_q = _dot_nt(eye_ref[...], sel_t.astype(dt))
        u = (sel_q - 1.0) * (-NEG)
        qa_ref[qt & 1] = jnp.where(lane_r < HEAD_DIM, qs, jnp.concatenate([u] * HPG, axis=0)).astype(dt)

    def item(qt, j):
        is_win = j > qt
        back = j - (qt + 1)
        chunk = jnp.where(is_win, qt - back, qt - j)
        krow = jnp.where(is_win, win_base + WINDOW + chunk * TQ, (chunk + 1) * TQ)
        vt_idx = jnp.where(is_win, nqt, 0) + jnp.maximum(chunk, 0)
        assert (T_D, T_A, T_W) == (0, 1, 2)
        bias_idx = jnp.where(is_win, back, jnp.where(j < T_W, j, T_FAR))
        branch = jnp.where(is_win, 1, 0)
        return pl.multiple_of(krow, TQ), vt_idx, bias_idx, branch

    def scores(qt, j, slot):
        krow, _, bias_idx, _ = item(qt, j)
        s = _dot_nt(ka_ref[pl.ds(krow, TQ), :], qa_ref[qt & 1]) + taba_ref[0, bias_idx]
        sc_refs[slot][...] = s
        cm_refs[slot][...] = jnp.broadcast_to(col_max(s), cm_refs[slot].shape)

    def accumulate(qt, j, slot):
        _, vt_idx, _, branch = item(qt, j)
        m_old = m_ref[branch]
        m_new = jnp.maximum(m_old, cm_refs[slot][...])
        m_ref[branch] = m_new
        p = jnp.exp2(sc_refs[slot][...] - m_new[0:1]).astype(dt)
        acc_ref[branch] = (acc_ref[branch] * jnp.exp2(m_old - m_new)[0:1]
                           + jnp.dot(vt_ref[0, 0, vt_idx], p, preferred_element_type=jnp.float32))

    def reset_state():
        m_ref[...] = jnp.full(m_ref.shape, NEG, jnp.float32)
        acc_ref[...] = jnp.zeros(acc_ref.shape, jnp.float32)

    def q_tile(qt, carry):
        s0 = pl.multiple_of(qt * TQ, TQ)
        n_pairs = (qt + 5) // 2
        reset_state()
        scores(qt, 0, 0)

        def pair(i):
            scores(qt, 2 * i + 1, 1)
            accumulate(qt, 2 * i, 0)
            scores(qt, 2 * i + 2, 0)
            accumulate(qt, 2 * i + 1, 1)

        def two_pairs(k, c):
            pair(2 * k)
            pair(2 * k + 1)
            return c

        n_loop = n_pairs - 1
        lax.fori_loop(0, n_loop // 2, two_pairs, 0)

        @pl.when(n_loop % 2 == 1)
        def _():
            pair(n_loop - 1)

        last = 2 * (n_pairs - 1)

        def tail(has_second):
            prep = prepare(jnp.minimum(qt + 1, nqt - 1))
            next(prep)
            if has_second:
                scores(qt, last + 1, 1)
            accumulate(qt, last, 0)
            next(prep)
            if has_second:
                accumulate(qt, last + 1, 1)
            for _ in prep:
                pass

            gt = gate_ref[0, pl.ds(s0, TQ), :].T
            o_c = oc_ref[qt & 1]
            mixed = []
            for h in range(HPG):
                cl = slice(h * TQ, (h + 1) * TQ)
                r0 = h * N_BRANCH
                parts = [gt[r0:r0 + 1] * o_c[:, cl]]
                for br in range(2):
                    parts.append((gt[r0 + 1 + br:r0 + 2 + br] / acc_ref[br, HEAD_DIM:HEAD_DIM + 1, cl])
                                 * acc_ref[br, 0:HEAD_DIM, cl])
                mixed.append(parts[0] + parts[1] + parts[2])
            out_ref[0, pl.ds(s0, TQ), :] = jnp.concatenate(mixed, axis=0).T.astype(out_ref.dtype)

        pl.when(qt % 2 == 0)(functools.partial(tail, True))
        pl.when(qt % 2 == 1)(functools.partial(tail, False))
        return carry

    for _ in prepare(0):
        pass
    lax.fori_loop(0, nqt, q_tile, 0)


def _nsa(q, ka, vt, kcvc, gates, taba, tabc, ovt, eye):
    b, s, _ = q.shape
    nb = s // SEL_BLOCK
    nqt = s // TQ
    assert HEAD_DIM + nb < LANES and s % TQ == 0
    kern = functools.partial(_nsa_kernel, seq=s, n_sel=min(TOP_N, nb))
    const = lambda shape: pl.BlockSpec(shape, lambda g, i: (0,) * len(shape))
    return pl.pallas_call(
        kern,
        grid=(N_KV, b),
        in_specs=[
            pl.BlockSpec((1, s, HPG * HEAD_DIM), lambda g, i: (i, 0, g)),
            pl.BlockSpec((1, 1, 2, s, LANES), lambda g, i: (i, g, 0, 0, 0)),
            pl.BlockSpec((1, 1, 2 * nqt, VT_ROWS, TQ), lambda g, i: (i, g, 0, 0, 0)),
            pl.BlockSpec((1, 1, s // CMP_STRIDE, LANES), lambda g, i: (i, g, 0, 0)),
            pl.BlockSpec((1, s, LANES), lambda g, i: (i, 0, g)),
            pl.BlockSpec((1,) + taba.shape[1:], lambda g, i: (g, 0, 0, 0)),
            pl.BlockSpec((1,) + tabc.shape[1:], lambda g, i: (g, 0, 0, 0)),
            const(ovt.shape), const(eye.shape),
        ],
        out_specs=pl.BlockSpec((1, s, HPG * HEAD_DIM), lambda g, i: (i, 0, g)),
        out_shape=jax.ShapeDtypeStruct((b, s, D_ATTN), MXU_DTYPE),
        scratch_shapes=[
            pltpu.VMEM((TQ + s + WINDOW + s, LANES), MXU_DTYPE),
            pltpu.VMEM((TQ, HPG * TQ), jnp.float32),
            pltpu.VMEM((TQ, HPG * TQ), jnp.float32),
            pltpu.VMEM((8, HPG * TQ), jnp.float32),
            pltpu.VMEM((8, HPG * TQ), jnp.float32),
            pltpu.VMEM((2, 8, HPG * TQ), jnp.float32),
            pltpu.VMEM((2, VT_ROWS, HPG * TQ), jnp.float32),
            pltpu.VMEM((nb, TQ), jnp.float32),
            pltpu.VMEM((2, HPG * TQ, LANES), MXU_DTYPE),
            pltpu.VMEM((2, HEAD_DIM, HPG * TQ), jnp.float32),
        ],
        compiler_params=pltpu.CompilerParams(
            dimension_semantics=("arbitrary", "arbitrary"), vmem_limit_bytes=VMEM_LIMIT),
        name="nsa",
    )(q, ka, vt, kcvc, gates, taba, tabc, ovt, eye)


def _layer_norm(x, g, b):
    mu = jnp.mean(x, axis=-1, keepdims=True)
    xc = x - mu
    var = jnp.mean(xc * xc, axis=-1, keepdims=True)
    return xc * lax.rsqrt(var + LN_EPS) * g + b


def _trunk_kernel(x_ref, yac_ref, yb_ref, woac_ref, wob_ref, g1_ref, b1_ref,
                  wg_ref, wu_ref, wd_ref, g2_ref, b2_ref, out_ref, *, fchunk):
    mix = (jnp.dot(yac_ref[...], woac_ref[...], preferred_element_type=jnp.float32)
           + jnp.dot(yb_ref[...], wob_ref[...], preferred_element_type=jnp.float32))
    x1 = _layer_norm(ALPHA * x_ref[...] + mix, g1_ref[...], b1_ref[...])
    x1b = x1.astype(MXU_DTYPE)
    ffn = jnp.zeros(x1.shape, jnp.float32)
    for c in range(D_FF // fchunk):
        cs = slice(c * fchunk, (c + 1) * fchunk)
        gate = jnp.dot(x1b, wg_ref[:, cs], preferred_element_type=jnp.float32)
        up = jnp.dot(x1b, wu_ref[:, cs], preferred_element_type=jnp.float32)
        act = (jax.nn.silu(gate) * up).astype(MXU_DTYPE)
        ffn = ffn + jnp.dot(act, wd_ref[cs, :], preferred_element_type=jnp.float32)
    out_ref[...] = _layer_norm(ALPHA * x1 + ffn, g2_ref[...], b2_ref[...])


def _trunk(x2, yac2, yb2, woac, wob, g1, b1, wg, wu, wd, g2, b2, tm, fchunk):
    n = x2.shape[0]
    kern = functools.partial(_trunk_kernel, fchunk=fchunk)
    const = lambda shape: pl.BlockSpec(shape, lambda i: (0,) * len(shape), pipeline_mode=pl.Buffered(1))
    rows = lambda width: pl.BlockSpec((tm, width), lambda i: (i, 0))
    return pl.pallas_call(
        kern,
        grid=(n // tm,),
        in_specs=[
            rows(D_MODEL), rows(D_CONV + D_POOL), rows(D_ATTN),
            const(woac.shape), const(wob.shape), const(g1.shape), const(b1.shape),
            const(wg.shape), const(wu.shape), const(wd.shape), const(g2.shape), const(b2.shape),
        ],
        out_specs=rows(D_MODEL),
        out_shape=jax.ShapeDtypeStruct((n, D_MODEL), jnp.float32),
        compiler_params=pltpu.CompilerParams(
            dimension_semantics=("arbitrary",), vmem_limit_bytes=VMEM_LIMIT),
        name="trunk",
    )(x2, yac2, yb2, woac, wob, g1, b1, wg, wu, wd, g2, b2)


def _proj_columns():
    src = np.full((N_PROJ,), -1, np.int64)
    scale = np.ones((N_PROJ,), np.float32)
    src[0:768] = np.arange(768)
    q0, kc0, vc0, ks0, vs0, kw0, vw0, gt0, xp0 = 768, 1280, 1408, 1536, 1664, 1792, 1920, 2048, 2072
    src[C_Q:C_Q + N_HEADS * HEAD_DIM] = q0 + np.arange(N_HEADS * HEAD_DIM)
    scale[C_Q:C_Q + N_HEADS * HEAD_DIM] = HEAD_DIM ** -0.5 * LOG2E
    d = np.arange(HEAD_DIM)
    for g in range(N_KV):
        base = C_CMP + g * LANES
        src[base:base + HEAD_DIM] = kc0 + g * HEAD_DIM + d
        src[base + HEAD_DIM:base + LANES] = vc0 + g * HEAD_DIM + d
        base = C_KV + g * LANES
        for k, col in enumerate((ks0, kw0)):
            src[base + k * HEAD_DIM:base + (k + 1) * HEAD_DIM] = col + g * HEAD_DIM + d
        ng = HPG * N_BRANCH
        src[C_GATE + g * LANES:C_GATE + g * LANES + ng] = gt0 + g * ng + np.arange(ng)
    src[C_POOL:C_POOL + D_POOL] = xp0 + np.arange(D_POOL)
    return src, scale


def _proj_runs():
    src, scale = _proj_columns()
    runs, i = [], 0
    while i < N_PROJ:
        j = i + 1
        while j < N_PROJ and scale[j] == scale[i] and (
                (src[i] < 0 and src[j] < 0) or (src[i] >= 0 and src[j] == src[j - 1] + 1)):
            j += 1
        runs.append((int(src[i]), j - i, float(scale[i])))
        i = j
    return runs


def _block_diag(blocks):
    n = len(blocks)
    r, c = blocks[0].shape
    out = jnp.zeros((n * r, n * c), blocks[0].dtype)
    for i, blk in enumerate(blocks):
        out = out.at[i * r:(i + 1) * r, i * c:(i + 1) * c].set(blk)
    return out


def _selection_constants(seq):
    nb = seq // SEL_BLOCK
    nc = (seq - CMP_BLOCK) // CMP_STRIDE + 1
    cmp_start = np.arange(nc) * CMP_STRIDE
    sel_start = np.arange(nb) * SEL_BLOCK
    ov = np.clip(np.minimum(cmp_start[:, None] + CMP_BLOCK, sel_start[None, :] + SEL_BLOCK)
                 - np.maximum(cmp_start[:, None], sel_start[None, :]), 0, None) / CMP_STRIDE
    ovt = np.zeros((nb, seq // CMP_STRIDE), np.float32)
    ovt[:, :nc] = ov.T
    return jnp.asarray(ovt), jnp.asarray(np.eye(TQ, dtype=np.float32), MXU_DTYPE)


def kernel(x, w_in, conv_w, cmp_pe, cmp_w1, cmp_w2, pool_w, pool_scale, w_out,
           ln1_g, ln1_b, w_gate, w_up, w_down, ln2_g, ln2_b, rel_bias):
    b, s, _ = x.shape
    assert s // CMP_STRIDE == LANES, "compressed keys are laid out as one 128-lane tile"
    tm_proj = 1024
    tm_trunk = 512
    nb_compress = 2 if b % 2 == 0 else 1
    fchunk = 256
    runs = _proj_runs()
    ovt, eye = _selection_constants(s)
    att_maps, cmp_maps = _bias_maps(s)
    taba = _bias_table(att_maps, rel_bias, True)
    tabc = _bias_table(cmp_maps, rel_bias, False)
    cdt = MXU_DTYPE

    for l in range(DEPTH):
        w_proj = jnp.concatenate(
            [jnp.zeros((D_MODEL, n), cdt) if a < 0 else (w_in[l][:, a:a + n] * sc).astype(cdt)
             for a, n, sc in runs], axis=1)
        pool_bd = _block_diag([pool_w[l, g] for g in range(len(POOL_WINDOWS))]).astype(cdt)
        vcols = np.concatenate([c0 + g * HEAD_DIM + np.arange(HEAD_DIM) for g in range(N_KV) for c0 in V_COLS])
        wvt = w_in[l][:, vcols].T.astype(cdt)
        yac, q, cmp_rows, ka, vt, gates = _proj_in(
            x, w_proj, wvt, conv_w[l], pool_bd, pool_scale[l][None, :], tm_proj)
        vt = vt.reshape(b, N_KV, -1, VT_ROWS, TQ)

        w1 = cmp_w1[l].reshape(2, CMP_BLOCK, HEAD_DIM, HEAD_DIM)
        zero = jnp.zeros_like(w1[0])
        w1bd = jnp.concatenate([jnp.concatenate([w1[0], zero], axis=2),
                                jnp.concatenate([zero, w1[1]], axis=2)], axis=1)
        w1bd = w1bd.reshape(CMP_BLOCK * LANES, LANES).astype(cdt)
        half = CMP_STRIDE * LANES
        pe = jnp.concatenate([cmp_pe[l, 0], cmp_pe[l, 1]], axis=1).reshape(1, CMP_BLOCK * LANES)
        w2bd = _block_diag([cmp_w2[l, 0], cmp_w2[l, 1]]).astype(cdt)
        kcvc = _compress(cmp_rows, pe[:, :half], pe[:, half:], w1bd[:half], w1bd[half:], w2bd, nb_compress)

        yb = _nsa(q, ka, vt, kcvc, gates, taba, tabc, ovt, eye)

        wo = w_out[l]
        woac = jnp.concatenate([wo[0:D_CONV], wo[D_CONV + D_ATTN:]], axis=0).astype(cdt)
        wob = wo[D_CONV:D_CONV + D_ATTN].astype(cdt)
        x = _trunk(
            x.reshape(b * s, D_MODEL), yac.reshape(b * s, -1), yb.reshape(b * s, -1),
            woac, wob, ln1_g[l][None, :], ln1_b[l][None, :],
            w_gate[l].astype(cdt), w_up[l].astype(cdt), w_down[l].astype(cdt),
            ln2_g[l][None, :], ln2_b[l][None, :], tm_trunk, fchunk).reshape(b, s, D_MODEL)
    return x
```

```python
import functools
import math

import numpy as np
import jax
import jax.numpy as jnp
from jax import lax
from jax.experimental import pallas as pl
from jax.experimental.pallas import tpu as pltpu

D_MODEL = 1024
D_CONV = 256
D_POOL = 256
POOL_WINDOWS = (2, 4, 8, 16)
POOL_GROUP = 64
D_ATTN = 512
HEAD_DIM = 64
N_HEADS = 8
N_KV = 2
HPG = 4
CMP_BLOCK = 32
CMP_STRIDE = 16
SEL_BLOCK = 64
TOP_N = 8
WINDOW = 512
N_BRANCH = 3
NUM_BUCKETS = 32
MAX_DISTANCE = 128
D_FF = 2816
DEPTH = 2
ALPHA = (2 * DEPTH) ** 0.25
LN_EPS = 1e-5
NEG = -1e30
FORCE = 1e6

LANES = 128
TQ = 256
VT_ROWS = 80
HALO = 16
VMEM_LIMIT = 56 * 1024 * 1024

MXU_DTYPE = jnp.bfloat16

C_CONV = 0
C_Q = 768
C_CMP = C_Q + N_HEADS * HEAD_DIM
C_KV = C_CMP + N_KV * LANES
C_POOL = C_KV + N_KV * LANES
C_GATE = C_POOL + D_POOL
N_PROJ = C_GATE + LANES
V_COLS = (1664, 1920)

T_D = 0
T_A = 1
T_W = 2
T_FAR = 3
LOG2E = math.log2(math.e)


def _t5_bucket_np(dist):
    n = np.maximum(dist, 0)
    max_exact = NUM_BUCKETS // 2
    nf = np.maximum(n, 1).astype(np.float64)
    val = np.log(nf / max_exact) / math.log(MAX_DISTANCE / max_exact) * (NUM_BUCKETS - max_exact)
    frac = np.abs(val - np.round(val))
    risky = (n > max_exact) & (n < MAX_DISTANCE) & (frac < 1e-6)
    assert not risky.any(), "bucket boundary too close to an integer"
    large = np.minimum(max_exact + np.floor(val + 1e-9).astype(np.int64), NUM_BUCKETS - 1)
    return np.where(n < max_exact, n, large).astype(np.int32)


def _bias_maps(seq):
    assert 2 * TQ == WINDOW
    nc = (seq - CMP_BLOCK) // CMP_STRIDE + 1
    i = np.arange(TQ)[None, :]

    def tile(dist, valid):
        return np.where(valid, _t5_bucket_np(dist), NUM_BUCKETS).astype(np.int32)

    j = np.arange(TQ)[:, None]
    att = []
    for chunks_back in range(3):
        dist = chunks_back * TQ + i - j
        att.append(tile(dist, (dist >= 0) & (dist < WINDOW)))
    att.append(np.full((TQ, TQ), NUM_BUCKETS - 1, np.int32))
    n = np.arange(seq // CMP_STRIDE)[:, None]
    cmp_maps = []
    for qt in range(seq // TQ):
        dist = TQ * qt + i - (CMP_STRIDE * n + CMP_BLOCK - 1)
        cmp_maps.append(tile(dist, (dist >= 0) & (n < nc)))
    return np.stack(att), np.stack(cmp_maps)


def _bias_table_kernel(map_ref, rb_ref, out_ref, *, relative):
    g = pl.program_id(0)
    m = map_ref[0]
    for h in range(HPG):
        head = g * HPG + h
        shift = rb_ref[NUM_BUCKETS - 1, head] if relative else 0.0
        acc = jnp.where(m == NUM_BUCKETS, NEG, 0.0).astype(jnp.float32)
        for b in range(NUM_BUCKETS):
            acc = jnp.where(m == b, (rb_ref[b, head] - shift) * LOG2E, acc)
        out_ref[0, 0, :, h * TQ:(h + 1) * TQ] = acc


def _bias_table(maps, rel_bias, relative):
    nt, rows, _ = maps.shape
    return pl.pallas_call(
        functools.partial(_bias_table_kernel, relative=relative),
        grid=(N_KV, nt),
        in_specs=[
            pl.BlockSpec((1, rows, TQ), lambda g, t: (t, 0, 0)),
            pl.BlockSpec(memory_space=pltpu.SMEM),
        ],
        out_specs=pl.BlockSpec((1, 1, rows, HPG * TQ), lambda g, t: (g, t, 0, 0)),
        out_shape=jax.ShapeDtypeStruct((N_KV, nt, rows, HPG * TQ), jnp.float32),
        name="bias_table",
    )(jnp.asarray(maps), rel_bias)


def _dot_nt(a, b, precision=None):
    return lax.dot_general(a, b, (((1,), (1,)), ((), ())), precision=precision,
                           preferred_element_type=jnp.float32)


def _proj_in_kernel(x_ref, w_ref, wvt_ref, convw_ref, poolw_ref, pscale_ref,
                    yac_ref, q_ref, cmp_ref, ka_ref, vt_ref, gate_ref, halo_ref, cmps_ref, *, tm, n_blocks):
    st = pl.program_id(1)
    first = st == 0
    xb = x_ref[0].astype(MXU_DTYPE)

    def proj(c0, width):
        return jnp.dot(xb, w_ref[:, c0:c0 + width], preferred_element_type=jnp.float32)

    @pl.when(first)
    def _():
        halo_ref[...] = jnp.zeros(halo_ref.shape, halo_ref.dtype)

    def with_halo(slot, cur):
        prev = halo_ref[slot]
        halo_ref[slot] = cur[tm - HALO:tm, :]
        return jnp.concatenate([prev, cur], axis=0)

    hc = proj(C_CONV, 3 * D_CONV)
    pooled = proj(C_POOL, D_POOL)
    q_ref[0] = proj(C_Q, N_HEADS * HEAD_DIM).astype(q_ref.dtype)
    hcmp = proj(C_CMP, N_KV * LANES)
    hk = proj(C_KV, N_KV * LANES)
    gate_ref[0] = jax.nn.sigmoid(proj(C_GATE, LANES))
    vts = _dot_nt(wvt_ref[...], xb)

    lane_k = lax.broadcasted_iota(jnp.int32, (tm, LANES), 1)
    blk_k = jnp.right_shift(st * tm + lax.broadcasted_iota(jnp.int32, (tm, LANES), 0), 6)
    onehot = jnp.where(lane_k == HEAD_DIM + blk_k, 1.0, 0.0)
    assert HEAD_DIM + n_blocks < LANES
    ones_rows = jnp.where(lax.broadcasted_iota(jnp.int32, (VT_ROWS - HEAD_DIM, TQ), 0) == 0, 1.0, 0.0)
    for g in range(N_KV):
        slab = hk[:, g * LANES:(g + 1) * LANES]
        ka_ref[0, g, 0] = jnp.where(lane_k < HEAD_DIM, slab, onehot).astype(ka_ref.dtype)
        ka_ref[0, g, 1] = jnp.where(lane_k < HEAD_DIM, pltpu.roll(slab, HEAD_DIM, 1), 0.0).astype(ka_ref.dtype)
        for br in range(2):
            r0 = (2 * g + br) * HEAD_DIM
            for c in range(tm // TQ):
                vt_ref[0, g, br, c] = jnp.concatenate(
                    [vts[r0:r0 + HEAD_DIM, c * TQ:(c + 1) * TQ], ones_rows], axis=0).astype(vt_ref.dtype)

    for g in range(N_KV):
        cmps_ref[g] = hcmp[:, g * LANES:(g + 1) * LANES]
        for r in range(CMP_STRIDE):
            cmp_ref[0, g, :, r * LANES:(r + 1) * LANES] = cmps_ref[g, pl.ds(r, tm // CMP_STRIDE, stride=CMP_STRIDE), :]

    u = hc[:, D_CONV:2 * D_CONV] * hc[:, 2 * D_CONV:3 * D_CONV]
    ue = with_halo(0, u)
    conv = (convw_ref[2:3, :] * ue + convw_ref[1:2, :] * pltpu.roll(ue, 1, 0)
            + convw_ref[0:1, :] * pltpu.roll(ue, 2, 0))
    ya = hc[:, 0:D_CONV] * conv[HALO:, :]
    yac_ref[0, :, 0:D_CONV] = ya.astype(yac_ref.dtype)

    pe = with_halo(1, pooled)
    lane = lax.broadcasted_iota(jnp.int32, pe.shape, 1)
    acc = pe + pltpu.roll(pe, 1, 0)
    for k, shift in enumerate((2, 4, 8)):
        acc = jnp.where(lane >= (k + 1) * POOL_GROUP, acc + pltpu.roll(acc, shift, 0), acc)
    sums = acc[HALO:, :]
    tpos = st * tm + lax.broadcasted_iota(jnp.int32, (tm, D_POOL), 0) + 1
    lane_t = lax.broadcasted_iota(jnp.int32, (tm, D_POOL), 1)
    win = jnp.left_shift(2, jnp.right_shift(lane_t, 6))
    cnt = jnp.minimum(tpos, win).astype(jnp.float32)
    d = sums / cnt - pooled
    yc = jnp.dot(d.astype(MXU_DTYPE), poolw_ref[...], preferred_element_type=jnp.float32) * pscale_ref[...]
    yac_ref[0, :, D_CONV:D_CONV + D_POOL] = yc.astype(yac_ref.dtype)


def _proj_in(x, w_proj, wvt, conv_w, pool_bd, pool_scale, tm):
    b, s, _ = x.shape
    kern = functools.partial(_proj_in_kernel, tm=tm, n_blocks=s // SEL_BLOCK)
    const = lambda shape: pl.BlockSpec(shape, lambda i, j: (0,) * len(shape))
    return pl.pallas_call(
        kern,
        grid=(b, s // tm),
        in_specs=[
            pl.BlockSpec((1, tm, D_MODEL), lambda i, j: (i, j, 0)),
            const((D_MODEL, N_PROJ)),
            const(wvt.shape),
            const((3, D_CONV)),
            const((D_POOL, D_POOL)),
            const((1, D_POOL)),
        ],
        out_specs=[
            pl.BlockSpec((1, tm, D_CONV + D_POOL), lambda i, j: (i, j, 0)),
            pl.BlockSpec((1, tm, N_HEADS * HEAD_DIM), lambda i, j: (i, j, 0)),
            pl.BlockSpec((1, N_KV, tm // CMP_STRIDE, CMP_STRIDE * LANES), lambda i, j: (i, 0, j, 0)),
            pl.BlockSpec((1, N_KV, 2, tm, LANES), lambda i, j: (i, 0, 0, j, 0)),
            pl.BlockSpec((1, N_KV, 2, tm // TQ, VT_ROWS, TQ), lambda i, j: (i, 0, 0, j, 0, 0)),
            pl.BlockSpec((1, tm, LANES), lambda i, j: (i, j, 0)),
        ],
        out_shape=[
            jax.ShapeDtypeStruct((b, s, D_CONV + D_POOL), MXU_DTYPE),
            jax.ShapeDtypeStruct((b, s, N_HEADS * HEAD_DIM), MXU_DTYPE),
            jax.ShapeDtypeStruct((b, N_KV, s // CMP_STRIDE, CMP_STRIDE * LANES), jnp.float32),
            jax.ShapeDtypeStruct((b, N_KV, 2, s, LANES), MXU_DTYPE),
            jax.ShapeDtypeStruct((b, N_KV, 2, s // TQ, VT_ROWS, TQ), MXU_DTYPE),
            jax.ShapeDtypeStruct((b, s, LANES), jnp.float32),
        ],
        scratch_shapes=[pltpu.VMEM((2, HALO, D_CONV), jnp.float32),
                        pltpu.VMEM((N_KV, tm, LANES), jnp.float32)],
        compiler_params=pltpu.CompilerParams(
            dimension_semantics=("arbitrary", "arbitrary"), vmem_limit_bytes=VMEM_LIMIT),
        name="proj_in",
    )(x, w_proj, wvt, conv_w, pool_bd, pool_scale)


def _compress_kernel(c_ref, pea_ref, peb_ref, w1a_ref, w1b_ref, w2_ref, out_ref):
    c = c_ref[...].reshape(-1, c_ref.shape[-1])
    rows = c.shape[0]
    a = jnp.dot((c + pea_ref[...]).astype(MXU_DTYPE), w1a_ref[...], preferred_element_type=jnp.float32)
    bm = jnp.dot((c + peb_ref[...]).astype(MXU_DTYPE), w1b_ref[...], preferred_element_type=jnp.float32)
    pre = a + pltpu.roll(bm, rows - 1, 0)
    hid = jax.nn.gelu(pre)
    out = jnp.dot(hid.astype(MXU_DTYPE), w2_ref[...], preferred_element_type=jnp.float32)
    out_ref[...] = out.astype(out_ref.dtype).reshape(out_ref.shape)


def _compress(cmp_rows, pea, peb, w1a, w1b, w2bd, nbatch):
    b, g, rows, width = cmp_rows.shape
    const = lambda shape: pl.BlockSpec(shape, lambda i: (0,) * len(shape))
    return pl.pallas_call(
        _compress_kernel,
        grid=(b // nbatch,),
        in_specs=[
            pl.BlockSpec((nbatch, g, rows, width), lambda i: (i, 0, 0, 0)),
            const((1, width)), const((1, width)),
            const((width, LANES)), const((width, LANES)), const((LANES, LANES)),
        ],
        out_specs=pl.BlockSpec((nbatch, g, rows, LANES), lambda i: (i, 0, 0, 0)),
        out_shape=jax.ShapeDtypeStruct((b, g, rows, LANES), MXU_DTYPE),
        compiler_params=pltpu.CompilerParams(
            dimension_semantics=("arbitrary",), vmem_limit_bytes=VMEM_LIMIT),
        name="compress",
    )(cmp_rows, pea, peb, w1a, w1b, w2bd)


def _nsa_kernel(q_ref, kin_ref, vt_ref, kc_ref, gate_ref, taba_ref, tabc_ref, ovt_ref, eye_ref,
                out_ref, ka_ref, sc0_ref, sc1_ref, cm0_ref, cm1_ref, m_ref, acc_ref, vs_ref,
                qa_ref, oc_ref,
                *, seq, n_sel):
    nqt = seq // TQ
    nb = seq // SEL_BLOCK
    pad_lane = HEAD_DIM + nb
    cols = HPG * TQ
    dt = ka_ref.dtype
    win_base = TQ + seq
    sc_refs = (sc0_ref, sc1_ref)
    cm_refs = (cm0_ref, cm1_ref)

    def pad_rows(n):
        lane_p = lax.broadcasted_iota(jnp.int32, (n, LANES), 1)
        return jnp.where(lane_p == pad_lane, 1.0, 0.0).astype(dt)

    ka_ref[0:TQ, :] = pad_rows(TQ)
    ka_ref[TQ:win_base, :] = kin_ref[0, 0, 0]
    ka_ref[win_base:win_base + WINDOW, :] = pad_rows(WINDOW)
    ka_ref[win_base + WINDOW:, :] = kin_ref[0, 0, 1]
    kcvc = kc_ref[0, 0]
    vct = kcvc.astype(jnp.float32).T[HEAD_DIM:, :].astype(dt)
    ovt = ovt_ref[...].astype(dt)

    def col_max(x):
        return jnp.max(x, axis=0, keepdims=True)

    def prepare(qt):
        s0 = pl.multiple_of(qt * TQ, TQ)
        heads = []
        for p in range(HPG // 2):
            grp = q_ref[0, pl.ds(s0, TQ), p * LANES:(p + 1) * LANES].astype(jnp.float32)
            heads += [grp, pltpu.roll(grp, HEAD_DIM, 1)]
        qs = jnp.concatenate(heads, axis=0)
        lane_r = lax.broadcasted_iota(jnp.int32, (cols, LANES), 1)
        q_cmp = jnp.where(lane_r < HEAD_DIM, qs, 0.0).astype(dt)

        sc = _dot_nt(kcvc, q_cmp) + tabc_ref[0, qt]
        yield
        mc = col_max(sc)
        ec = jnp.where(sc > 0.5 * NEG, jnp.exp2(sc - mc), 0.0)
        lc = jnp.sum(ec, axis=0, keepdims=True)
        pc = (ec / jnp.where(lc > 0.0, lc, 1.0)).astype(dt)
        oc_ref[qt & 1] = jnp.dot(vct, pc, preferred_element_type=jnp.float32)

        imp4 = jnp.dot(ovt, pc, preferred_element_type=jnp.float32)
        yield
        imp = functools.reduce(lambda a, b: a + b, [imp4[:, h * TQ:(h + 1) * TQ] for h in range(HPG)])
        jb = lax.broadcasted_iota(jnp.int32, (nb, TQ), 0)
        qb = qt * (TQ // SEL_BLOCK) + jnp.right_shift(lax.broadcasted_iota(jnp.int32, (nb, TQ), 1), 6)
        valid = jb <= qb
        forced = (jb == 0) | (jb == qb) | (jb == qb - 1)
        v = jnp.where(valid, imp + jnp.where(forced, FORCE, 0.0), NEG)
        vs_ref[...] = v
        sub = lax.broadcasted_iota(jnp.int32, (8, TQ), 0)
        v_grp = [v[8 * k:8 * k + 8] for k in range(nb // 8)]
        r_grp = [jnp.zeros((8, TQ), jnp.int32) for _ in v_grp]
        for jp in range(nb):
            row = vs_ref[jp:jp + 1, :]
            for k, vk in enumerate(v_grp):
                if k > jp // 8:
                    ahead = row >= vk
                elif k < jp // 8:
                    ahead = row > vk
                else:
                    ahead = (row > vk) | ((row == vk) & (sub > jp % 8))
                r_grp[k] = r_grp[k] + ahead.astype(jnp.int32)
        rank = jnp.concatenate(r_grp, axis=0)
        sel_t = jnp.where(valid & (rank < n_sel), 1.0, 0.0)
        sel_t = jnp.concatenate([jnp.zeros((HEAD_DIM, TQ), jnp.float32), sel_t,
                                 jnp.zeros((LANES - HEAD_DIM - nb, TQ), jnp.float32)], axis=0)
        sel_q = _dot_nt(eye_ref[...], sel_t.astype(dt))
        u = (sel_q - 1.0) * (-NEG)
        qa_ref[qt & 1] = jnp.where(lane_r < HEAD_DIM, qs, jnp.concatenate([u] * HPG, axis=0)).astype(dt)

    def item(qt, j):
        is_win = j > qt
        back = j - (qt + 1)
        chunk = jnp.where(is_win, qt - back, qt - j)
        krow = jnp.where(is_win, win_base + WINDOW + chunk * TQ, (chunk + 1) * TQ)
        vt_idx = jnp.where(is_win, nqt, 0) + jnp.maximum(chunk, 0)
        assert (T_D, T_A, T_W) == (0, 1, 2)
        bias_idx = jnp.where(is_win, back, jnp.where(j < T_W, j, T_FAR))
        branch = jnp.where(is_win, 1, 0)
        return pl.multiple_of(krow, TQ), vt_idx, bias_idx, branch

    def scores(qt, j, slot):
        krow, _, bias_idx, _ = item(qt, j)
        s = _dot_nt(ka_ref[pl.ds(krow, TQ), :], qa_ref[qt & 1]) + taba_ref[0, bias_idx]
        sc_refs[slot][...] = s
        cm_refs[slot][...] = jnp.broadcast_to(col_max(s), cm_refs[slot].shape)

    def accumulate(qt, j, slot):
        _, vt_idx, _, branch = item(qt, j)
        m_old = m_ref[branch]
        m_new = jnp.maximum(m_old, cm_refs[slot][...])
        m_ref[branch] = m_new
        p = jnp.exp2(sc_refs[slot][...] - m_new[0:1]).astype(dt)
        acc_ref[branch] = (acc_ref[branch] * jnp.exp2(m_old - m_new)[0:1]
                           + jnp.dot(vt_ref[0, 0, vt_idx], p, preferred_element_type=jnp.float32))

    def reset_state():
        m_ref[...] = jnp.full(m_ref.shape, NEG, jnp.float32)
        acc_ref[...] = jnp.zeros(acc_ref.shape, jnp.float32)

    def q_tile(qt, carry):
        s0 = pl.multiple_of(qt * TQ, TQ)
        n_pairs = (qt + 5) // 2
        reset_state()
        scores(qt, 0, 0)

        def pair(i):
            scores(qt, 2 * i + 1, 1)
            accumulate(qt, 2 * i, 0)
            scores(qt, 2 * i + 2, 0)
            accumulate(qt, 2 * i + 1, 1)

        def two_pairs(k, c):
            pair(2 * k)
            pair(2 * k + 1)
            return c

        n_loop = n_pairs - 1
        lax.fori_loop(0, n_loop // 2, two_pairs, 0)

        @pl.when(n_loop % 2 == 1)
        def _():
            pair(n_loop - 1)

        last = 2 * (n_pairs - 1)

        def tail(has_second):
            prep = prepare(jnp.minimum(qt + 1, nqt - 1))
            next(prep)
            if has_second:
                scores(qt, last + 1, 1)
            accumulate(qt, last, 0)
            next(prep)
            if has_second:
                accumulate(qt, last + 1, 1)
            for _ in prep:
                pass

            gt = gate_ref[0, pl.ds(s0, TQ), :].T
            first_group = pl.program_id(0) == 0
            o_c = oc_ref[qt & 1]
            mixed = []
            for h in range(HPG):
                cl = slice(h * TQ, (h + 1) * TQ)

                def gate(br):
                    r0, r1 = h * N_BRANCH + br, (HPG + h) * N_BRANCH + br
                    return jnp.where(first_group, gt[r0:r0 + 1], gt[r1:r1 + 1])

                parts = [gate(0) * o_c[:, cl]]
                for br in range(2):
                    parts.append((gate(1 + br) / acc_ref[br, HEAD_DIM:HEAD_DIM + 1, cl])
                                 * acc_ref[br, 0:HEAD_DIM, cl])
                mixed.append(parts[0] + parts[1] + parts[2])
            out_ref[0, pl.ds(s0, TQ), :] = jnp.concatenate(mixed, axis=0).T.astype(out_ref.dtype)

        pl.when(qt % 2 == 0)(functools.partial(tail, True))
        pl.when(qt % 2 == 1)(functools.partial(tail, False))
        return carry

    for _ in prepare(0):
        pass
    lax.fori_loop(0, nqt, q_tile, 0)


def _nsa(q, ka, vt, kcvc, gates, taba, tabc, ovt, eye):
    b, s, _ = q.shape
    nb = s // SEL_BLOCK
    nqt = s // TQ
    assert HEAD_DIM + nb < LANES and s % TQ == 0
    kern = functools.partial(_nsa_kernel, seq=s, n_sel=min(TOP_N, nb))
    const = lambda shape: pl.BlockSpec(shape, lambda g, i: (0,) * len(shape))
    return pl.pallas_call(
        kern,
        grid=(N_KV, b),
        in_specs=[
            pl.BlockSpec((1, s, HPG * HEAD_DIM), lambda g, i: (i, 0, g)),
            pl.BlockSpec((1, 1, 2, s, LANES), lambda g, i: (i, g, 0, 0, 0)),
            pl.BlockSpec((1, 1, 2 * nqt, VT_ROWS, TQ), lambda g, i: (i, g, 0, 0, 0)),
            pl.BlockSpec((1, 1, s // CMP_STRIDE, LANES), lambda g, i: (i, g, 0, 0)),
            pl.BlockSpec((1, s, LANES), lambda g, i: (i, 0, 0)),
            pl.BlockSpec((1,) + taba.shape[1:], lambda g, i: (g, 0, 0, 0)),
            pl.BlockSpec((1,) + tabc.shape[1:], lambda g, i: (g, 0, 0, 0)),
            const(ovt.shape), const(eye.shape),
        ],
        out_specs=pl.BlockSpec((1, s, HPG * HEAD_DIM), lambda g, i: (i, 0, g)),
        out_shape=jax.ShapeDtypeStruct((b, s, D_ATTN), MXU_DTYPE),
        scratch_shapes=[
            pltpu.VMEM((TQ + s + WINDOW + s, LANES), MXU_DTYPE),
            pltpu.VMEM((TQ, HPG * TQ), jnp.float32),
            pltpu.VMEM((TQ, HPG * TQ), jnp.float32),
            pltpu.VMEM((8, HPG * TQ), jnp.float32),
            pltpu.VMEM((8, HPG * TQ), jnp.float32),
            pltpu.VMEM((2, 8, HPG * TQ), jnp.float32),
            pltpu.VMEM((2, VT_ROWS, HPG * TQ), jnp.float32),
            pltpu.VMEM((nb, TQ), jnp.float32),
            pltpu.VMEM((2, HPG * TQ, LANES), MXU_DTYPE),
            pltpu.VMEM((2, HEAD_DIM, HPG * TQ), jnp.float32),
        ],
        compiler_params=pltpu.CompilerParams(
            dimension_semantics=("arbitrary", "arbitrary"), vmem_limit_bytes=VMEM_LIMIT),
        name="nsa",
    )(q, ka, vt, kcvc, gates, taba, tabc, ovt, eye)


def _layer_norm(x, g, b):
    mu = jnp.mean(x, axis=-1, keepdims=True)
    xc = x - mu
    var = jnp.mean(xc * xc, axis=-1, keepdims=True)
    return xc * lax.rsqrt(var + LN_EPS) * g + b


def _trunk_kernel(x_ref, yac_ref, yb_ref, woac_ref, wob_ref, g1_ref, b1_ref,
                  wg_ref, wu_ref, wd_ref, g2_ref, b2_ref, out_ref, *, fchunk):
    mix = (jnp.dot(yac_ref[...], woac_ref[...], preferred_element_type=jnp.float32)
           + jnp.dot(yb_ref[...], wob_ref[...], preferred_element_type=jnp.float32))
    x1 = _layer_norm(ALPHA * x_ref[...] + mix, g1_ref[...], b1_ref[...])
    x1b = x1.astype(MXU_DTYPE)
    ffn = jnp.zeros(x1.shape, jnp.float32)
    for c in range(D_FF // fchunk):
        cs = slice(c * fchunk, (c + 1) * fchunk)
        gate = jnp.dot(x1b, wg_ref[:, cs], preferred_element_type=jnp.float32)
        up = jnp.dot(x1b, wu_ref[:, cs], preferred_element_type=jnp.float32)
        act = (jax.nn.silu(gate) * up).astype(MXU_DTYPE)
        ffn = ffn + jnp.dot(act, wd_ref[cs, :], preferred_element_type=jnp.float32)
    out_ref[...] = _layer_norm(ALPHA * x1 + ffn, g2_ref[...], b2_ref[...])


def _trunk(x2, yac2, yb2, woac, wob, g1, b1, wg, wu, wd, g2, b2, tm, fchunk):
    n = x2.shape[0]
    kern = functools.partial(_trunk_kernel, fchunk=fchunk)
    const = lambda shape: pl.BlockSpec(shape, lambda i: (0,) * len(shape), pipeline_mode=pl.Buffered(1))
    rows = lambda width: pl.BlockSpec((tm, width), lambda i: (i, 0))
    return pl.pallas_call(
        kern,
        grid=(n // tm,),
        in_specs=[
            rows(D_MODEL), rows(D_CONV + D_POOL), rows(D_ATTN),
            const(woac.shape), const(wob.shape), const(g1.shape), const(b1.shape),
            const(wg.shape), const(wu.shape), const(wd.shape), const(g2.shape), const(b2.shape),
        ],
        out_specs=rows(D_MODEL),
        out_shape=jax.ShapeDtypeStruct((n, D_MODEL), jnp.float32),
        compiler_params=pltpu.CompilerParams(
            dimension_semantics=("arbitrary",), vmem_limit_bytes=VMEM_LIMIT),
        name="trunk",
    )(x2, yac2, yb2, woac, wob, g1, b1, wg, wu, wd, g2, b2)


def _proj_columns():
    src = np.full((N_PROJ,), -1, np.int64)
    scale = np.ones((N_PROJ,), np.float32)
    src[0:768] = np.arange(768)
    q0, kc0, vc0, ks0, kw0, gt0, xp0 = 768, 1280, 1408, 1536, 1792, 2048, 2072
    src[C_Q:C_Q + N_HEADS * HEAD_DIM] = q0 + np.arange(N_HEADS * HEAD_DIM)
    scale[C_Q:C_Q + N_HEADS * HEAD_DIM] = HEAD_DIM ** -0.5 * LOG2E
    d = np.arange(HEAD_DIM)
    for g in range(N_KV):
        base = C_CMP + g * LANES
        src[base:base + HEAD_DIM] = kc0 + g * HEAD_DIM + d
        src[base + HEAD_DIM:base + LANES] = vc0 + g * HEAD_DIM + d
        base = C_KV + g * LANES
        for k, col in enumerate((ks0, kw0)):
            src[base + k * HEAD_DIM:base + (k + 1) * HEAD_DIM] = col + g * HEAD_DIM + d
    src[C_GATE:C_GATE + N_HEADS * N_BRANCH] = gt0 + np.arange(N_HEADS * N_BRANCH)
    src[C_POOL:C_POOL + D_POOL] = xp0 + np.arange(D_POOL)
    return src, scale


def _proj_runs():
    src, scale = _proj_columns()
    runs, i = [], 0
    while i < N_PROJ:
        j = i + 1
        while j < N_PROJ and scale[j] == scale[i] and (
                (src[i] < 0 and src[j] < 0) or (src[i] >= 0 and src[j] == src[j - 1] + 1)):
            j += 1
        runs.append((int(src[i]), j - i, float(scale[i])))
        i = j
    return runs


def _block_diag(blocks):
    n = len(blocks)
    r, c = blocks[0].shape
    out = jnp.zeros((n * r, n * c), blocks[0].dtype)
    for i, blk in enumerate(blocks):
        out = out.at[i * r:(i + 1) * r, i * c:(i + 1) * c].set(blk)
    return out


def _selection_constants(seq):
    nb = seq // SEL_BLOCK
    nc = (seq - CMP_BLOCK) // CMP_STRIDE + 1
    cmp_start = np.arange(nc) * CMP_STRIDE
    sel_start = np.arange(nb) * SEL_BLOCK
    ov = np.clip(np.minimum(cmp_start[:, None] + CMP_BLOCK, sel_start[None, :] + SEL_BLOCK)
                 - np.maximum(cmp_start[:, None], sel_start[None, :]), 0, None) / CMP_STRIDE
    ovt = np.zeros((nb, seq // CMP_STRIDE), np.float32)
    ovt[:, :nc] = ov.T
    return jnp.asarray(ovt), jnp.asarray(np.eye(TQ, dtype=np.float32), MXU_DTYPE)


def kernel(x, w_in, conv_w, cmp_pe, cmp_w1, cmp_w2, pool_w, pool_scale, w_out,
           ln1_g, ln1_b, w_gate, w_up, w_down, ln2_g, ln2_b, rel_bias):
    b, s, _ = x.shape
    assert s // CMP_STRIDE == LANES, "compressed keys are laid out as one 128-lane tile"
    tm_proj = 1024
    tm_trunk = 512
    nb_compress = 2 if b % 2 == 0 else 1
    fchunk = 256
    runs = _proj_runs()
    ovt, eye = _selection_constants(s)
    att_maps, cmp_maps = _bias_maps(s)
    taba = _bias_table(att_maps, rel_bias, True)
    tabc = _bias_table(cmp_maps, rel_bias, False)
    cdt = MXU_DTYPE

    for l in range(DEPTH):
        w_proj = jnp.concatenate(
            [jnp.zeros((D_MODEL, n), cdt) if a < 0 else (w_in[l][:, a:a + n] * sc).astype(cdt)
             for a, n, sc in runs], axis=1)
        pool_bd = _block_diag([pool_w[l, g] for g in range(len(POOL_WINDOWS))]).astype(cdt)
        vcols = np.concatenate([c0 + g * HEAD_DIM + np.arange(HEAD_DIM) for g in range(N_KV) for c0 in V_COLS])
        wvt = w_in[l][:, vcols].T.astype(cdt)
        yac, q, cmp_rows, ka, vt, gates = _proj_in(
            x, w_proj, wvt, conv_w[l], pool_bd, pool_scale[l][None, :], tm_proj)
        vt = vt.reshape(b, N_KV, -1, VT_ROWS, TQ)

        w1 = cmp_w1[l].reshape(2, CMP_BLOCK, HEAD_DIM, HEAD_DIM)
        zero = jnp.zeros_like(w1[0])
        w1bd = jnp.concatenate([jnp.concatenate([w1[0], zero], axis=2),
                                jnp.concatenate([zero, w1[1]], axis=2)], axis=1)
        w1bd = w1bd.reshape(CMP_BLOCK * LANES, LANES).astype(cdt)
        half = CMP_STRIDE * LANES
        pe = jnp.concatenate([cmp_pe[l, 0], cmp_pe[l, 1]], axis=1).reshape(1, CMP_BLOCK * LANES)
        w2bd = _block_diag([cmp_w2[l, 0], cmp_w2[l, 1]]).astype(cdt)
        kcvc = _compress(cmp_rows, pe[:, :half], pe[:, half:], w1bd[:half], w1bd[half:], w2bd, nb_compress)

        yb = _nsa(q, ka, vt, kcvc, gates, taba, tabc, ovt, eye)

        wo = w_out[l]
        woac = jnp.concatenate([wo[0:D_CONV], wo[D_CONV + D_ATTN:]], axis=0).astype(cdt)
        wob = wo[D_CONV:D_CONV + D_ATTN].astype(cdt)
        x = _trunk(
            x.reshape(b * s, D_MODEL), yac.reshape(b * s, -1), yb.reshape(b * s, -1),
            woac, wob, ln1_g[l][None, :], ln1_b[l][None, :],
            w_gate[l].astype(cdt), w_up[l].astype(cdt), w_down[l].astype(cdt),
            ln2_g[l][None, :], ln2_b[l][None, :], tm_trunk, fchunk).reshape(b, s, D_MODEL)
    return x
```

```python
import functools
import math

import numpy as np
import jax
import jax.numpy as jnp
from jax import lax
from jax.experimental import pallas as pl
from jax.experimental.pallas import tpu as pltpu

D_MODEL = 1024
D_CONV = 256
D_POOL = 256
POOL_WINDOWS = (2, 4, 8, 16)
POOL_GROUP = 64
D_ATTN = 512
HEAD_DIM = 64
N_HEADS = 8
N_KV = 2
HPG = 4
CMP_BLOCK = 32
CMP_STRIDE = 16
SEL_BLOCK = 64
TOP_N = 8
WINDOW = 512
N_BRANCH = 3
NUM_BUCKETS = 32
MAX_DISTANCE = 128
D_FF = 2816
DEPTH = 2
ALPHA = (2 * DEPTH) ** 0.25
LN_EPS = 1e-5
NEG = -1e30
FORCE = 1e6

LANES = 128
TQ = 256
VT_ROWS = 80
HALO = 16
VMEM_LIMIT = 56 * 1024 * 1024

MXU_DTYPE = jnp.bfloat16

C_CONV = 0
C_Q = 768
C_CMP = C_Q + N_HEADS * HEAD_DIM
C_KV = C_CMP + N_KV * LANES
C_POOL = C_KV + N_KV * LANES
C_GATE = C_POOL + D_POOL
N_PROJ = C_GATE + LANES
V_COLS = (1664, 1920)

T_D = 0
T_A = 1
T_W = 2
T_FAR = 3
LOG2E = math.log2(math.e)


def _t5_bucket_np(dist):
    n = np.maximum(dist, 0)
    max_exact = NUM_BUCKETS // 2
    nf = np.maximum(n, 1).astype(np.float64)
    val = np.log(nf / max_exact) / math.log(MAX_DISTANCE / max_exact) * (NUM_BUCKETS - max_exact)
    frac = np.abs(val - np.round(val))
    risky = (n > max_exact) & (n < MAX_DISTANCE) & (frac < 1e-6)
    assert not risky.any(), "bucket boundary too close to an integer"
    large = np.minimum(max_exact + np.floor(val + 1e-9).astype(np.int64), NUM_BUCKETS - 1)
    return np.where(n < max_exact, n, large).astype(np.int32)


def _bias_maps(seq):
    assert 2 * TQ == WINDOW
    nc = (seq - CMP_BLOCK) // CMP_STRIDE + 1
    i = np.arange(TQ)[None, :]

    def tile(dist, valid):
        return np.where(valid, _t5_bucket_np(dist), NUM_BUCKETS).astype(np.int32)

    j = np.arange(TQ)[:, None]
    att = []
    for chunks_back in range(3):
        dist = chunks_back * TQ + i - j
        att.append(tile(dist, (dist >= 0) & (dist < WINDOW)))
    att.append(np.full((TQ, TQ), NUM_BUCKETS - 1, np.int32))
    n = np.arange(seq // CMP_STRIDE)[:, None]
    cmp_maps = []
    for qt in range(seq // TQ):
        dist = TQ * qt + i - (CMP_STRIDE * n + CMP_BLOCK - 1)
        cmp_maps.append(tile(dist, (dist >= 0) & (n < nc)))
    return np.stack(att), np.stack(cmp_maps)


def _bias_table_kernel(map_ref, rb_ref, out_ref, *, relative):
    g = pl.program_id(0)
    m = map_ref[0]
    for h in range(HPG):
        head = g * HPG + h
        shift = rb_ref[NUM_BUCKETS - 1, head] if relative else 0.0
        acc = jnp.where(m == NUM_BUCKETS, NEG, 0.0).astype(jnp.float32)
        for b in range(NUM_BUCKETS):
            acc = jnp.where(m == b, (rb_ref[b, head] - shift) * LOG2E, acc)
        out_ref[0, 0, :, h * TQ:(h + 1) * TQ] = acc


def _bias_table(maps, rel_bias, relative):
    nt, rows, _ = maps.shape
    return pl.pallas_call(
        functools.partial(_bias_table_kernel, relative=relative),
        grid=(N_KV, nt),
        in_specs=[
            pl.BlockSpec((1, rows, TQ), lambda g, t: (t, 0, 0)),
            pl.BlockSpec(memory_space=pltpu.SMEM),
        ],
        out_specs=pl.BlockSpec((1, 1, rows, HPG * TQ), lambda g, t: (g, t, 0, 0)),
        out_shape=jax.ShapeDtypeStruct((N_KV, nt, rows, HPG * TQ), jnp.float32),
        name="bias_table",
    )(jnp.asarray(maps), rel_bias)


def _dot_nt(a, b, precision=None):
    return lax.dot_general(a, b, (((1,), (1,)), ((), ())), precision=precision,
                           preferred_element_type=jnp.float32)


def _proj_in_kernel(x_ref, w_ref, wvt_ref, convw_ref, poolw_ref, pscale_ref,
                    yac_ref, q_ref, cmp_ref, ka_ref, vt_ref, gate_ref, halo_ref, cmps_ref, *, tm, n_blocks):
    st = pl.program_id(1)
    first = st == 0
    xb = x_ref[0].astype(MXU_DTYPE)

    def proj(c0, width):
        return jnp.dot(xb, w_ref[:, c0:c0 + width], preferred_element_type=jnp.float32)

    @pl.when(first)
    def _():
        halo_ref[...] = jnp.zeros(halo_ref.shape, halo_ref.dtype)

    def with_halo(slot, cur):
        prev = halo_ref[slot]
        halo_ref[slot] = cur[tm - HALO:tm, :]
        return jnp.concatenate([prev, cur], axis=0)

    hc = proj(C_CONV, 3 * D_CONV)
    pooled = proj(C_POOL, D_POOL)
    q_ref[0] = proj(C_Q, N_HEADS * HEAD_DIM).astype(q_ref.dtype)
    hcmp = proj(C_CMP, N_KV * LANES)
    hk = proj(C_KV, N_KV * LANES)
    gate_ref[0] = jax.nn.sigmoid(proj(C_GATE, LANES))
    vts = _dot_nt(wvt_ref[...], xb)

    lane_k = lax.broadcasted_iota(jnp.int32, (tm, LANES), 1)
    blk_k = jnp.right_shift(st * tm + lax.broadcasted_iota(jnp.int32, (tm, LANES), 0), 6)
    onehot = jnp.where(lane_k == HEAD_DIM + blk_k, 1.0, 0.0)
    assert HEAD_DIM + n_blocks < LANES
    ones_rows = jnp.where(lax.broadcasted_iota(jnp.int32, (VT_ROWS - HEAD_DIM, TQ), 0) == 0, 1.0, 0.0)
    for g in range(N_KV):
        slab = hk[:, g * LANES:(g + 1) * LANES]
        ka_ref[0, g, 0] = jnp.where(lane_k < HEAD_DIM, slab, onehot).astype(ka_ref.dtype)
        ka_ref[0, g, 1] = jnp.where(lane_k < HEAD_DIM, pltpu.roll(slab, HEAD_DIM, 1), 0.0).astype(ka_ref.dtype)
        for br in range(2):
            r0 = (2 * g + br) * HEAD_DIM
            for c in range(tm // TQ):
                vt_ref[0, g, br, c] = jnp.concatenate(
                    [vts[r0:r0 + HEAD_DIM, c * TQ:(c + 1) * TQ], ones_rows], axis=0).astype(vt_ref.dtype)

    for g in range(N_KV):
        cmps_ref[g] = hcmp[:, g * LANES:(g + 1) * LANES]
        for r in range(CMP_STRIDE):
            cmp_ref[0, g, :, r * LANES:(r + 1) * LANES] = cmps_ref[g, pl.ds(r, tm // CMP_STRIDE, stride=CMP_STRIDE), :]

    u = hc[:, D_CONV:2 * D_CONV] * hc[:, 2 * D_CONV:3 * D_CONV]
    ue = with_halo(0, u)
    conv = (convw_ref[2:3, :] * ue + convw_ref[1:2, :] * pltpu.roll(ue, 1, 0)
            + convw_ref[0:1, :] * pltpu.roll(ue, 2, 0))
    ya = hc[:, 0:D_CONV] * conv[HALO:, :]
    yac_ref[0, :, 0:D_CONV] = ya.astype(yac_ref.dtype)

    pe = with_halo(1, pooled)
    lane = lax.broadcasted_iota(jnp.int32, pe.shape, 1)
    acc = pe + pltpu.roll(pe, 1, 0)
    for k, shift in enumerate((2, 4, 8)):
        acc = jnp.where(lane >= (k + 1) * POOL_GROUP, acc + pltpu.roll(acc, shift, 0), acc)
    sums = acc[HALO:, :]
    tpos = st * tm + lax.broadcasted_iota(jnp.int32, (tm, D_POOL), 0) + 1
    lane_t = lax.broadcasted_iota(jnp.int32, (tm, D_POOL), 1)
    win = jnp.left_shift(2, jnp.right_shift(lane_t, 6))
    cnt = jnp.minimum(tpos, win).astype(jnp.float32)
    d = sums / cnt - pooled
    yc = jnp.dot(d.astype(MXU_DTYPE), poolw_ref[...], preferred_element_type=jnp.float32) * pscale_ref[...]
    yac_ref[0, :, D_CONV:D_CONV + D_POOL] = yc.astype(yac_ref.dtype)


def _proj_in(x, w_proj, wvt, conv_w, pool_bd, pool_scale, tm):
    b, s, _ = x.shape
    kern = functools.partial(_proj_in_kernel, tm=tm, n_blocks=s // SEL_BLOCK)
    const = lambda shape: pl.BlockSpec(shape, lambda i, j: (0,) * len(shape))
    return pl.pallas_call(
        kern,
        grid=(b, s // tm),
        in_specs=[
            pl.BlockSpec((1, tm, D_MODEL), lambda i, j: (i, j, 0)),
            const((D_MODEL, N_PROJ)),
            const(wvt.shape),
            const((3, D_CONV)),
            const((D_POOL, D_POOL)),
            const((1, D_POOL)),
        ],
        out_specs=[
            pl.BlockSpec((1, tm, D_CONV + D_POOL), lambda i, j: (i, j, 0)),
            pl.BlockSpec((1, tm, N_HEADS * HEAD_DIM), lambda i, j: (i, j, 0)),
            pl.BlockSpec((1, N_KV, tm // CMP_STRIDE, CMP_STRIDE * LANES), lambda i, j: (i, 0, j, 0)),
            pl.BlockSpec((1, N_KV, 2, tm, LANES), lambda i, j: (i, 0, 0, j, 0)),
            pl.BlockSpec((1, N_KV, 2, tm // TQ, VT_ROWS, TQ), lambda i, j: (i, 0, 0, j, 0, 0)),
            pl.BlockSpec((1, tm, LANES), lambda i, j: (i, j, 0)),
        ],
        out_shape=[
            jax.ShapeDtypeStruct((b, s, D_CONV + D_POOL), MXU_DTYPE),
            jax.ShapeDtypeStruct((b, s, N_HEADS * HEAD_DIM), MXU_DTYPE),
            jax.ShapeDtypeStruct((b, N_KV, s // CMP_STRIDE, CMP_STRIDE * LANES), jnp.float32),
            jax.ShapeDtypeStruct((b, N_KV, 2, s, LANES), MXU_DTYPE),
            jax.ShapeDtypeStruct((b, N_KV, 2, s // TQ, VT_ROWS, TQ), MXU_DTYPE),
            jax.ShapeDtypeStruct((b, s, LANES), jnp.float32),
        ],
        scratch_shapes=[pltpu.VMEM((2, HALO, D_CONV), jnp.float32),
                        pltpu.VMEM((N_KV, tm, LANES), jnp.float32)],
        compiler_params=pltpu.CompilerParams(
            dimension_semantics=("arbitrary", "arbitrary"), vmem_limit_bytes=VMEM_LIMIT),
        name="proj_in",
    )(x, w_proj, wvt, conv_w, pool_bd, pool_scale)


def _compress_kernel(c_ref, pea_ref, peb_ref, w1a_ref, w1b_ref, w2_ref, out_ref):
    c = c_ref[...].reshape(-1, c_ref.shape[-1])
    rows = c.shape[0]
    a = jnp.dot((c + pea_ref[...]).astype(MXU_DTYPE), w1a_ref[...], preferred_element_type=jnp.float32)
    bm = jnp.dot((c + peb_ref[...]).astype(MXU_DTYPE), w1b_ref[...], preferred_element_type=jnp.float32)
    pre = a + pltpu.roll(bm, rows - 1, 0)
    hid = jax.nn.gelu(pre)
    out = jnp.dot(hid.astype(MXU_DTYPE), w2_ref[...], preferred_element_type=jnp.float32)
    out_ref[...] = out.astype(out_ref.dtype).reshape(out_ref.shape)


def _compress(cmp_rows, pea, peb, w1a, w1b, w2bd, nbatch):
    b, g, rows, width = cmp_rows.shape
    const = lambda shape: pl.BlockSpec(shape, lambda i: (0,) * len(shape))
    return pl.pallas_call(
        _compress_kernel,
        grid=(b // nbatch,),
        in_specs=[
            pl.BlockSpec((nbatch, g, rows, width), lambda i: (i, 0, 0, 0)),
            const((1, width)), const((1, width)),
            const((width, LANES)), const((width, LANES)), const((LANES, LANES)),
        ],
        out_specs=pl.BlockSpec((nbatch, g, rows, LANES), lambda i: (i, 0, 0, 0)),
        out_shape=jax.ShapeDtypeStruct((b, g, rows, LANES), MXU_DTYPE),
        compiler_params=pltpu.CompilerParams(
            dimension_semantics=("arbitrary",), vmem_limit_bytes=VMEM_LIMIT),
        name="compress",
    )(cmp_rows, pea, peb, w1a, w1b, w2bd)


def _nsa_kernel(q_ref, kin_ref, vt_ref, kc_ref, gate_ref, taba_ref, tabc_ref, ovt_ref, eye_ref,
                out_ref, ka_ref, sc0_ref, sc1_ref, cm0_ref, cm1_ref, m_ref, acc_ref, vs_ref,
                qa_ref, oc_ref,
                *, seq, n_sel):
    nqt = seq // TQ
    nb = seq // SEL_BLOCK
    pad_lane = HEAD_DIM + nb
    cols = HPG * TQ
    dt = ka_ref.dtype
    win_base = TQ + seq
    sc_refs = (sc0_ref, sc1_ref)
    cm_refs = (cm0_ref, cm1_ref)

    def pad_rows(n):
        lane_p = lax.broadcasted_iota(jnp.int32, (n, LANES), 1)
        return jnp.where(lane_p == pad_lane, 1.0, 0.0).astype(dt)

    ka_ref[0:TQ, :] = pad_rows(TQ)
    ka_ref[TQ:win_base, :] = kin_ref[0, 0, 0]
    ka_ref[win_base:win_base + WINDOW, :] = pad_rows(WINDOW)
    ka_ref[win_base + WINDOW:, :] = kin_ref[0, 0, 1]
    kcvc = kc_ref[0, 0]
    vct = kcvc.astype(jnp.float32).T[HEAD_DIM:, :].astype(dt)
    ovt = ovt_ref[...].astype(dt)

    def col_max(x):
        return jnp.max(x, axis=0, keepdims=True)

    def prepare(qt):
        s0 = pl.multiple_of(qt * TQ, TQ)
        heads = []
        for p in range(HPG // 2):
            grp = q_ref[0, pl.ds(s0, TQ), p * LANES:(p + 1) * LANES].astype(jnp.float32)
            heads += [grp, pltpu.roll(grp, HEAD_DIM, 1)]
        qs = jnp.concatenate(heads, axis=0)
        lane_r = lax.broadcasted_iota(jnp.int32, (cols, LANES), 1)
        q_cmp = jnp.where(lane_r < HEAD_DIM, qs, 0.0).astype(dt)

        sc = _dot_nt(kcvc, q_cmp) + tabc_ref[0, qt]
        yield
        mc = col_max(sc)
        ec = jnp.where(sc > 0.5 * NEG, jnp.exp2(sc - mc), 0.0)
        lc = jnp.sum(ec, axis=0, keepdims=True)
        pc = (ec / jnp.where(lc > 0.0, lc, 1.0)).astype(dt)
        oc_ref[qt & 1] = jnp.dot(vct, pc, preferred_element_type=jnp.float32)

        imp4 = jnp.dot(ovt, pc, preferred_element_type=jnp.float32)
        yield
        imp = functools.reduce(lambda a, b: a + b, [imp4[:, h * TQ:(h + 1) * TQ] for h in range(HPG)])
        jb = lax.broadcasted_iota(jnp.int32, (nb, TQ), 0)
        qb = qt * (TQ // SEL_BLOCK) + jnp.right_shift(lax.broadcasted_iota(jnp.int32, (nb, TQ), 1), 6)
        valid = jb <= qb
        forced = (jb == 0) | (jb == qb) | (jb == qb - 1)
        v = jnp.where(valid, imp + jnp.where(forced, FORCE, 0.0), NEG)
        vs_ref[...] = v
        sub = lax.broadcasted_iota(jnp.int32, (8, TQ), 0)
        v_grp = [v[8 * k:8 * k + 8] for k in range(nb // 8)]
        r_grp = [jnp.zeros((8, TQ), jnp.int32) for _ in v_grp]
        for jp in range(nb):
            row = vs_ref[jp:jp + 1, :]
            for k, vk in enumerate(v_grp):
                if k > jp // 8:
                    ahead = row >= vk
                elif k < jp // 8:
                    ahead = row > vk
                else:
                    ahead = (row > vk) | ((row == vk) & (sub > jp % 8))
                r_grp[k] = r_grp[k] + ahead.astype(jnp.int32)
        rank = jnp.concatenate(r_grp, axis=0)
        sel_t = jnp.where(valid & (rank < n_sel), 1.0, 0.0)
        sel_t = jnp.concatenate([jnp.zeros((HEAD_DIM, TQ), jnp.float32), sel_t,
                                 jnp.zeros((LANES - HEAD_DIM - nb, TQ), jnp.float32)], axis=0)
        sel_q = _dot_nt(eye_ref[...], sel_t.astype(dt))
        u = (sel_q - 1.0) * (-NEG)
        qa_ref[qt & 1] = jnp.where(lane_r < HEAD_DIM, qs, jnp.concatenate([u] * HPG, axis=0)).astype(dt)

    def item(qt, j):
        is_win = j > qt
        back = j - (qt + 1)
        chunk = jnp.where(is_win, qt - back, qt - j)
        krow = jnp.where(is_win, win_base + WINDOW + chunk * TQ, (chunk + 1) * TQ)
        vt_idx = jnp.where(is_win, nqt, 0) + jnp.maximum(chunk, 0)
        assert (T_D, T_A, T_W) == (0, 1, 2)
        bias_idx = jnp.where(is_win, back, jnp.where(j < T_W, j, T_FAR))
        branch = jnp.where(is_win, 1, 0)
        return pl.multiple_of(krow, TQ), vt_idx, bias_idx, branch

    def scores(qt, j, slot):
        krow, _, bias_idx, _ = item(qt, j)
        s = _dot_nt(ka_ref[pl.ds(krow, TQ), :], qa_ref[qt & 1]) + taba_ref[0, bias_idx]
        sc_refs[slot][...] = s
        cm_refs[slot][...] = jnp.broadcast_to(col_max(s), cm_refs[slot].shape)

    def accumulate(qt, j, slot):
        _, vt_idx, _, branch = item(qt, j)
        m_old = m_ref[branch]
        m_new = jnp.maximum(m_old, cm_refs[slot][...])
        m_ref[branch] = m_new
        p = jnp.exp2(sc_refs[slot][...] - m_new[0:1]).astype(dt)
        acc_ref[branch] = (acc_ref[branch] * jnp.exp2(m_old - m_new)[0:1]
                           + jnp.dot(vt_ref[0, 0, vt_idx], p, preferred_element_type=jnp.float32))

    def reset_state():
        m_ref[...] = jnp.full(m_ref.shape, NEG, jnp.float32)
        acc_ref[...] = jnp.zeros(acc_ref.shape, jnp.float32)

    def q_tile(qt, carry):
        s0 = pl.multiple_of(qt * TQ, TQ)
        n_pairs = (qt + 5) // 2
        reset_state()
        scores(qt, 0, 0)

        def pair(i):
            scores(qt, 2 * i + 1, 1)
            accumulate(qt, 2 * i, 0)
            scores(qt, 2 * i + 2, 0)
            accumulate(qt, 2 * i + 1, 1)

        def two_pairs(k, c):
            pair(2 * k)
            pair(2 * k + 1)
            return c

        n_loop = n_pairs - 1
        lax.fori_loop(0, n_loop // 2, two_pairs, 0)
        last = 2 * (n_pairs - 1)

        def tail(has_odd_pair, has_second):
            if has_odd_pair:
                pair(n_loop - 1)
            prep = prepare(jnp.minimum(qt + 1, nqt - 1))
            next(prep)
            if has_second:
                scores(qt, last + 1, 1)
            accumulate(qt, last, 0)
            next(prep)
            if has_second:
                accumulate(qt, last + 1, 1)
            for _ in prep:
                pass

            gt = gate_ref[0, pl.ds(s0, TQ), :].T
            first_group = pl.program_id(0) == 0
            o_c = oc_ref[qt & 1]
            mixed = []
            for h in range(HPG):
                cl = slice(h * TQ, (h + 1) * TQ)

                def gate(br):
                    r0, r1 = h * N_BRANCH + br, (HPG + h) * N_BRANCH + br
                    return jnp.where(first_group, gt[r0:r0 + 1], gt[r1:r1 + 1])

                parts = [gate(0) * o_c[:, cl]]
                for br in range(2):
                    parts.append((gate(1 + br) / acc_ref[br, HEAD_DIM:HEAD_DIM + 1, cl])
                                 * acc_ref[br, 0:HEAD_DIM, cl])
                mixed.append(parts[0] + parts[1] + parts[2])
            out_ref[0, pl.ds(s0, TQ), :] = jnp.concatenate(mixed, axis=0).T.astype(out_ref.dtype)

        for odd_pair in (False, True):
            for second in (False, True):
                cond = (n_loop % 2 == int(odd_pair)) & (qt % 2 == (0 if second else 1))
                pl.when(cond)(functools.partial(tail, odd_pair, second))
        return carry

    for _ in prepare(0):
        pass
    lax.fori_loop(0, nqt, q_tile, 0)


def _nsa(q, ka, vt, kcvc, gates, taba, tabc, ovt, eye):
    b, s, _ = q.shape
    nb = s // SEL_BLOCK
    nqt = s // TQ
    assert HEAD_DIM + nb < LANES and s % TQ == 0
    kern = functools.partial(_nsa_kernel, seq=s, n_sel=min(TOP_N, nb))
    const = lambda shape: pl.BlockSpec(shape, lambda g, i: (0,) * len(shape))
    return pl.pallas_call(
        kern,
        grid=(N_KV, b),
        in_specs=[
            pl.BlockSpec((1, s, HPG * HEAD_DIM), lambda g, i: (i, 0, g)),
            pl.BlockSpec((1, 1, 2, s, LANES), lambda g, i: (i, g, 0, 0, 0)),
            pl.BlockSpec((1, 1, 2 * nqt, VT_ROWS, TQ), lambda g, i: (i, g, 0, 0, 0)),
            pl.BlockSpec((1, 1, s // CMP_STRIDE, LANES), lambda g, i: (i, g, 0, 0)),
            pl.BlockSpec((1, s, LANES), lambda g, i: (i, 0, 0)),
            pl.BlockSpec((1,) + taba.shape[1:], lambda g, i: (g, 0, 0, 0)),
            pl.BlockSpec((1,) + tabc.shape[1:], lambda g, i: (g, 0, 0, 0)),
            const(ovt.shape), const(eye.shape),
        ],
        out_specs=pl.BlockSpec((1, s, HPG * HEAD_DIM), lambda g, i: (i, 0, g)),
        out_shape=jax.ShapeDtypeStruct((b, s, D_ATTN), MXU_DTYPE),
        scratch_shapes=[
            pltpu.VMEM((TQ + s + WINDOW + s, LANES), MXU_DTYPE),
            pltpu.VMEM((TQ, HPG * TQ), jnp.float32),
            pltpu.VMEM((TQ, HPG * TQ), jnp.float32),
            pltpu.VMEM((8, HPG * TQ), jnp.float32),
            pltpu.VMEM((8, HPG * TQ), jnp.float32),
            pltpu.VMEM((2, 8, HPG * TQ), jnp.float32),
            pltpu.VMEM((2, VT_ROWS, HPG * TQ), jnp.float32),
            pltpu.VMEM((nb, TQ), jnp.float32),
            pltpu.VMEM((2, HPG * TQ, LANES), MXU_DTYPE),
            pltpu.VMEM((2, HEAD_DIM, HPG * TQ), jnp.float32),
        ],
        compiler_params=pltpu.CompilerParams(
            dimension_semantics=("arbitrary", "arbitrary"), vmem_limit_bytes=VMEM_LIMIT),
        name="nsa",
    )(q, ka, vt, kcvc, gates, taba, tabc, ovt, eye)


def _layer_norm(x, g, b):
    mu = jnp.mean(x, axis=-1, keepdims=True)
    xc = x - mu
    var = jnp.mean(xc * xc, axis=-1, keepdims=True)
    return xc * lax.rsqrt(var + LN_EPS) * g + b


def _trunk_kernel(x_ref, yac_ref, yb_ref, woac_ref, wob_ref, g1_ref, b1_ref,
                  wg_ref, wu_ref, wd_ref, g2_ref, b2_ref, out_ref, *, fchunk):
    mix = (jnp.dot(yac_ref[...], woac_ref[...], preferred_element_type=jnp.float32)
           + jnp.dot(yb_ref[...], wob_ref[...], preferred_element_type=jnp.float32))
    x1 = _layer_norm(ALPHA * x_ref[...] + mix, g1_ref[...], b1_ref[...])
    x1b = x1.astype(MXU_DTYPE)
    ffn = jnp.zeros(x1.shape, jnp.float32)
    for c in range(D_FF // fchunk):
        cs = slice(c * fchunk, (c + 1) * fchunk)
        gate = jnp.dot(x1b, wg_ref[:, cs], preferred_element_type=jnp.float32)
        up = jnp.dot(x1b, wu_ref[:, cs], preferred_element_type=jnp.float32)
        act = (jax.nn.silu(gate) * up).astype(MXU_DTYPE)
        ffn = ffn + jnp.dot(act, wd_ref[cs, :], preferred_element_type=jnp.float32)
    out_ref[...] = _layer_norm(ALPHA * x1 + ffn, g2_ref[...], b2_ref[...])


def _trunk(x2, yac2, yb2, woac, wob, g1, b1, wg, wu, wd, g2, b2, tm, fchunk):
    n = x2.shape[0]
    kern = functools.partial(_trunk_kernel, fchunk=fchunk)
    const = lambda shape: pl.BlockSpec(shape, lambda i: (0,) * len(shape), pipeline_mode=pl.Buffered(1))
    rows = lambda width: pl.BlockSpec((tm, width), lambda i: (i, 0))
    return pl.pallas_call(
        kern,
        grid=(n // tm,),
        in_specs=[
            rows(D_MODEL), rows(D_CONV + D_POOL), rows(D_ATTN),
            const(woac.shape), const(wob.shape), const(g1.shape), const(b1.shape),
            const(wg.shape), const(wu.shape), const(wd.shape), const(g2.shape), const(b2.shape),
        ],
        out_specs=rows(D_MODEL),
        out_shape=jax.ShapeDtypeStruct((n, D_MODEL), jnp.float32),
        compiler_params=pltpu.CompilerParams(
            dimension_semantics=("arbitrary",), vmem_limit_bytes=VMEM_LIMIT),
        name="trunk",
    )(x2, yac2, yb2, woac, wob, g1, b1, wg, wu, wd, g2, b2)


def _proj_columns():
    src = np.full((N_PROJ,), -1, np.int64)
    scale = np.ones((N_PROJ,), np.float32)
    src[0:768] = np.arange(768)
    q0, kc0, vc0, ks0, kw0, gt0, xp0 = 768, 1280, 1408, 1536, 1792, 2048, 2072
    src[C_Q:C_Q + N_HEADS * HEAD_DIM] = q0 + np.arange(N_HEADS * HEAD_DIM)
    scale[C_Q:C_Q + N_HEADS * HEAD_DIM] = HEAD_DIM ** -0.5 * LOG2E
    d = np.arange(HEAD_DIM)
    for g in range(N_KV):
        base = C_CMP + g * LANES
        src[base:base + HEAD_DIM] = kc0 + g * HEAD_DIM + d
        src[base + HEAD_DIM:base + LANES] = vc0 + g * HEAD_DIM + d
        base = C_KV + g * LANES
        for k, col in enumerate((ks0, kw0)):
            src[base + k * HEAD_DIM:base + (k + 1) * HEAD_DIM] = col + g * HEAD_DIM + d
    src[C_GATE:C_GATE + N_HEADS * N_BRANCH] = gt0 + np.arange(N_HEADS * N_BRANCH)
    src[C_POOL:C_POOL + D_POOL] = xp0 + np.arange(D_POOL)
    return src, scale


def _proj_runs():
    src, scale = _proj_columns()
    runs, i = [], 0
    while i < N_PROJ:
        j = i + 1
        while j < N_PROJ and scale[j] == scale[i] and (
                (src[i] < 0 and src[j] < 0) or (src[i] >= 0 and src[j] == src[j - 1] + 1)):
            j += 1
        runs.append((int(src[i]), j - i, float(scale[i])))
        i = j
    return runs


def _block_diag(blocks):
    n = len(blocks)
    r, c = blocks[0].shape
    out = jnp.zeros((n * r, n * c), blocks[0].dtype)
    for i, blk in enumerate(blocks):
        out = out.at[i * r:(i + 1) * r, i * c:(i + 1) * c].set(blk)
    return out


def _selection_constants(seq):
    nb = seq // SEL_BLOCK
    nc = (seq - CMP_BLOCK) // CMP_STRIDE + 1
    cmp_start = np.arange(nc) * CMP_STRIDE
    sel_start = np.arange(nb) * SEL_BLOCK
    ov = np.clip(np.minimum(cmp_start[:, None] + CMP_BLOCK, sel_start[None, :] + SEL_BLOCK)
                 - np.maximum(cmp_start[:, None], sel_start[None, :]), 0, None) / CMP_STRIDE
    ovt = np.zeros((nb, seq // CMP_STRIDE), np.float32)
    ovt[:, :nc] = ov.T
    return jnp.asarray(ovt), jnp.asarray(np.eye(TQ, dtype=np.float32), MXU_DTYPE)


def kernel(x, w_in, conv_w, cmp_pe, cmp_w1, cmp_w2, pool_w, pool_scale, w_out,
           ln1_g, ln1_b, w_gate, w_up, w_down, ln2_g, ln2_b, rel_bias):
    b, s, _ = x.shape
    assert s // CMP_STRIDE == LANES, "compressed keys are laid out as one 128-lane tile"
    tm_proj = 1024
    tm_trunk = 512
    nb_compress = 2 if b % 2 == 0 else 1
    fchunk = 256
    runs = _proj_runs()
    ovt, eye = _selection_constants(s)
    att_maps, cmp_maps = _bias_maps(s)
    taba = _bias_table(att_maps, rel_bias, True)
    tabc = _bias_table(cmp_maps, rel_bias, False)
    cdt = MXU_DTYPE

    for l in range(DEPTH):
        w_proj = jnp.concatenate(
            [jnp.zeros((D_MODEL, n), cdt) if a < 0 else (w_in[l][:, a:a + n] * sc).astype(cdt)
             for a, n, sc in runs], axis=1)
        pool_bd = _block_diag([pool_w[l, g] for g in range(len(POOL_WINDOWS))]).astype(cdt)
        vcols = np.concatenate([c0 + g * HEAD_DIM + np.arange(HEAD_DIM) for g in range(N_KV) for c0 in V_COLS])
        wvt = w_in[l][:, vcols].T.astype(cdt)
        yac, q, cmp_rows, ka, vt, gates = _proj_in(
            x, w_proj, wvt, conv_w[l], pool_bd, pool_scale[l][None, :], tm_proj)
        vt = vt.reshape(b, N_KV, -1, VT_ROWS, TQ)

        w1 = cmp_w1[l].reshape(2, CMP_BLOCK, HEAD_DIM, HEAD_DIM)
        zero = jnp.zeros_like(w1[0])
        w1bd = jnp.concatenate([jnp.concatenate([w1[0], zero], axis=2),
                                jnp.concatenate([zero, w1[1]], axis=2)], axis=1)
        w1bd = w1bd.reshape(CMP_BLOCK * LANES, LANES).astype(cdt)
        half = CMP_STRIDE * LANES
        pe = jnp.concatenate([cmp_pe[l, 0], cmp_pe[l, 1]], axis=1).reshape(1, CMP_BLOCK * LANES)
        w2bd = _block_diag([cmp_w2[l, 0], cmp_w2[l, 1]]).astype(cdt)
        kcvc = _compress(cmp_rows, pe[:, :half], pe[:, half:], w1bd[:half], w1bd[half:], w2bd, nb_compress)

        yb = _nsa(q, ka, vt, kcvc, gates, taba, tabc, ovt, eye)

        wo = w_out[l]
        woac = jnp.concatenate([wo[0:D_CONV], wo[D_CONV + D_ATTN:]], axis=0).astype(cdt)
        wob = wo[D_CONV:D_CONV + D_ATTN].astype(cdt)
        x = _trunk(
            x.reshape(b * s, D_MODEL), yac.reshape(b * s, -1), yb.reshape(b * s, -1),
            woac, wob, ln1_g[l][None, :], ln1_b[l][None, :],
            w_gate[l].astype(cdt), w_up[l].astype(cdt), w_down[l].astype(cdt),
            ln2_g[l][None, :], ln2_b[l][None, :], tm_trunk, fchunk).reshape(b, s, D_MODEL)
    return x
```
